```python
import jax
import jax.numpy as jnp
from jax import lax
import numpy as np

D_MODEL = 1024
BATCH = 16
SEQ = 256
DEPTH = 4
DEC_BATCH = 8
DEC_SEQ = 4096
PAST_LEN = 256

GRID_W = 64
EPS = 1e-6
ROPE_BASE = 10000.0
BLOCK = 128
NEG_INF = -1e30
D_RNN = 1024
RNN_BLOCKS = 8
RNN_BW = D_RNN // RNN_BLOCKS
CONV_W = 4
CONV_LEFT = (CONV_W - 1) // 2
LRU_C = 8.0
MLA_HEADS = 8
MLA_NOPE = 64
MLA_ROPE = 32
MLA_V = 64
Q_LORA = 384
KV_LORA = 256
MLA_WIDTH = MLA_HEADS * MLA_V
SWA_HEADS = 8
SWA_KV_HEADS = 2
SWA_GROUPS = SWA_HEADS // SWA_KV_HEADS
SWA_HD = 64
WINDOW = 128
SWA_WIDTH = SWA_HEADS * SWA_HD
N_BRANCH = 3
D_IN = (2 * D_RNN + Q_LORA + KV_LORA + MLA_ROPE + MLA_WIDTH
        + SWA_HEADS * SWA_HD + 2 * SWA_KV_HEADS * SWA_HD + SWA_WIDTH + N_BRANCH * D_MODEL)

kernel_name = "hybrid_flow_backbone_step"


def _in_split_points():
    sizes = (D_RNN, D_RNN, Q_LORA, KV_LORA, MLA_ROPE, MLA_WIDTH, SWA_HEADS * SWA_HD,
             SWA_KV_HEADS * SWA_HD, SWA_KV_HEADS * SWA_HD, SWA_WIDTH, N_BRANCH * D_MODEL)
    pts, acc = [], 0
    for s in sizes[:-1]:
        acc += s
        pts.append(acc)
    return pts


def rmsnorm(x, g):
    xf = x.astype(jnp.float32)
    y = xf * lax.rsqrt(jnp.mean(xf * xf, axis=-1, keepdims=True) + EPS)
    return (y * g.astype(jnp.float32)).astype(x.dtype)


def axial_rope_tables(n_tok, rot_dim):
    rows = n_tok // GRID_W
    row = jnp.repeat(jnp.arange(rows, dtype=jnp.float32), GRID_W)
    col = jnp.tile(jnp.arange(GRID_W, dtype=jnp.float32), rows)
    half = rot_dim // 2
    freqs = ROPE_BASE ** (-jnp.arange(0, half, 2, dtype=jnp.float32) / half)
    ang_r = row[:, None] * freqs[None, :]
    ang_c = col[:, None] * freqs[None, :]
    ang = jnp.concatenate([ang_r, ang_r, ang_c, ang_c], axis=-1)
    return jnp.cos(ang), jnp.sin(ang)


def apply_rope(x, cos, sin):
    r = x.shape[-1]
    q = r // 4
    rot = jnp.concatenate([-x[..., q:2 * q], x[..., :q], -x[..., 3 * q:], x[..., 2 * q:3 * q]], axis=-1)
    shape = (1, x.shape[1]) + (1,) * (x.ndim - 3) + (r,)
    return x * cos.reshape(shape).astype(x.dtype) + rot * sin.reshape(shape).astype(x.dtype)


def centred_conv(x, w, b):
    L = x.shape[1]
    xp = jnp.pad(x, ((0, 0), (CONV_LEFT, CONV_W - 1 - CONV_LEFT), (0, 0)))
    acc = xp[:, 0:L] * w[0]
    for k in range(1, CONV_W):
        acc = acc + xp[:, k:k + L] * w[k]
    return acc + b


def rglru_scan(x, h0, w_a, b_a, w_i, b_i, lam, reverse):
    B, L, _ = x.shape
    xb = x.reshape(B, L, RNN_BLOCKS, RNN_BW)
    r = jax.nn.sigmoid((jnp.einsum("blnc,ncd->blnd", xb, w_a).reshape(B, L, D_RNN) + b_a).astype(jnp.float32))
    i = jax.nn.sigmoid((jnp.einsum("blnc,ncd->blnd", xb, w_i).reshape(B, L, D_RNN) + b_i).astype(jnp.float32))
    log_a = LRU_C * r * jax.nn.log_sigmoid(lam.astype(jnp.float32))
    a = jnp.exp(log_a)
    u = jnp.sqrt(-jnp.expm1(2.0 * log_a)) * i * x.astype(jnp.float32)

    def combine(e1, e2):
        return (e1[0] * e2[0], e2[0] * e1[1] + e2[1])

    a_cum, h = lax.associative_scan(combine, (a, u), reverse=reverse, axis=1)
    return (h + a_cum * h0.astype(jnp.float32)[:, None, :]).astype(x.dtype)


def rglru_branch(x_rnn, h0_f, h0_b, p):
    xc = centred_conv(x_rnn, p["conv_w"], p["conv_b"])
    h_f = rglru_scan(xc, h0_f, p["lru_wa"][0], p["lru_ba"][0], p["lru_wi"][0], p["lru_bi"][0], p["lru_lam"][0], False)
    h_b = rglru_scan(xc, h0_b, p["lru_wa"][1], p["lru_ba"][1], p["lru_wi"][1], p["lru_bi"][1], p["lru_lam"][1], True)
    return h_f, h_b


def mla_q(c_q, p):
    B, L = c_q.shape[:2]
    q = (rmsnorm(c_q, p["mla_q_norm"]) @ p["mla_w_uq"]).reshape(B, L, MLA_HEADS, MLA_NOPE + MLA_ROPE)
    return q[..., :MLA_NOPE], q[..., MLA_NOPE:]


def mla_kv_up(ckv, p):
    B, L = ckv.shape[:2]
    kv = (ckv @ p["mla_w_ukv"]).reshape(B, L, MLA_HEADS, MLA_NOPE + MLA_V)
    return kv[..., :MLA_NOPE], kv[..., MLA_NOPE:]


def mla_attend(q_nope, q_rope, k_nope, k_rope, v):
    B, Lq = q_nope.shape[:2]
    nb = Lq // BLOCK
    scale = (MLA_NOPE + MLA_ROPE) ** -0.5
    qn = q_nope.reshape(B, nb, BLOCK, MLA_HEADS, MLA_NOPE).swapaxes(0, 1)
    qr = q_rope.reshape(B, nb, BLOCK, MLA_HEADS, MLA_ROPE).swapaxes(0, 1)

    def block(args):
        qn_b, qr_b = args
        s = jnp.einsum("bqhd,bkhd->bhqk", qn_b, k_nope) + jnp.einsum("bqhr,bkr->bhqk", qr_b, k_rope)
        pr = jax.nn.softmax(s.astype(jnp.float32) * scale, axis=-1).astype(v.dtype)
        return jnp.einsum("bhqk,bkhd->bqhd", pr, v)

    o = lax.map(block, (qn, qr))
    return o.swapaxes(0, 1).reshape(B, Lq, MLA_WIDTH)


def sink_softmax(s, sink):
    sk = jnp.broadcast_to(sink.astype(jnp.float32).reshape(1, SWA_KV_HEADS, SWA_GROUPS, 1, 1), s.shape[:-1] + (1,))
    return jax.nn.softmax(jnp.concatenate([s, sk], axis=-1), axis=-1)[..., :-1]


def swa_context(q, k, v, sink):
    B, L = q.shape[:2]
    nb = L // BLOCK
    scale = SWA_HD ** -0.5
    qb = q.reshape(B, nb, BLOCK, SWA_KV_HEADS, SWA_GROUPS, SWA_HD).swapaxes(0, 1)

    def block(q_b):
        s = jnp.einsum("bqhgd,bkhd->bhgqk", q_b, k).astype(jnp.float32) * scale
        pr = sink_softmax(s, sink).astype(v.dtype)
        return jnp.einsum("bhgqk,bkhd->bqhgd", pr, v)

    o = lax.map(block, qb)
    return o.swapaxes(0, 1).reshape(B, L, SWA_WIDTH)


def swa_latent(q, k, v, k_ctx, v_ctx, sink):
    B, N = q.shape[:2]
    Lc = k_ctx.shape[1]
    nb = N // BLOCK
    scale = SWA_HD ** -0.5
    pad = ((0, 0), (BLOCK, BLOCK), (0, 0), (0, 0))
    kp = jnp.pad(k, pad)
    vp = jnp.pad(v, pad)
    qb = q.reshape(B, nb, BLOCK, SWA_KV_HEADS, SWA_GROUPS, SWA_HD).swapaxes(0, 1)
    offs = jnp.arange(3 * BLOCK)
    band = jnp.abs(jnp.arange(BLOCK)[:, None] + BLOCK - offs[None, :]) <= WINDOW
    ctx_ok = jnp.ones((BLOCK, Lc), dtype=bool)

    def block(args):
        n, q_b = args
        start = n * BLOCK
        k_b = lax.dynamic_slice_in_dim(kp, start, 3 * BLOCK, axis=1)
        v_b = lax.dynamic_slice_in_dim(vp, start, 3 * BLOCK, axis=1)
        kpos = start - BLOCK + offs
        valid = jnp.concatenate([band & ((kpos >= 0) & (kpos < N))[None, :], ctx_ok], axis=1)
        k_all = jnp.concatenate([k_b, k_ctx], axis=1)
        v_all = jnp.concatenate([v_b, v_ctx], axis=1)
        s = jnp.einsum("bqhgd,bkhd->bhgqk", q_b, k_all).astype(jnp.float32) * scale
        s = jnp.where(valid, s, NEG_INF)
        pr = sink_softmax(s, sink).astype(v.dtype)
        return jnp.einsum("bhgqk,bkhd->bqhgd", pr, v_all)

    o = lax.map(block, (jnp.arange(nb), qb))
    return o.swapaxes(0, 1).reshape(B, N, SWA_WIDTH)


def mix_inputs(x, cond, p):
    mod = jax.nn.silu(cond) @ p["w_mod"] + p["b_mod"]
    shift, scale, gate = jnp.split(mod, 3, axis=-1)
    h = rmsnorm(x, p["g_norm"]) * (1 + scale[:, None, :]) + shift[:, None, :]
    parts = jnp.split(h @ p["w_in"], _in_split_points(), axis=-1)
    return gate, parts


def merge_branches(y_rnn, y_mla, y_swa, g_rnn, g_mla, g_swa, merge_logits, p):
    B, L, _ = y_rnn.shape
    m = jax.nn.sigmoid(merge_logits.astype(jnp.float32)).astype(y_rnn.dtype).reshape(B, L, N_BRANCH, D_MODEL)
    u = (m[:, :, 0] * ((y_rnn * jax.nn.silu(g_rnn)) @ p["w_br_rnn"])
         + m[:, :, 1] * ((y_mla * jax.nn.silu(g_mla)) @ p["w_br_mla"])
         + m[:, :, 2] * ((y_swa * jax.nn.silu(g_swa)) @ p["w_br_swa"]))
    return u @ p["w_out"]


def context_layer(x, cond, p):
    B, L, _ = x.shape
    gate, (x_rnn, g_rnn, c_q, c_kv, k_rope, g_mla, q_s, k_s, v_s, g_swa, merge_logits) = mix_inputs(x, cond, p)
    zeros = jnp.zeros((B, D_RNN), x.dtype)
    h_f, h_b = rglru_branch(x_rnn, zeros, zeros, p)
    y_rnn = h_f + h_b
    ckv = rmsnorm(c_kv, p["mla_kv_norm"])
    q_nope, q_rope = mla_q(c_q, p)
    k_nope, v_m = mla_kv_up(ckv, p)
    y_mla = mla_attend(q_nope, q_rope, k_nope, k_rope, v_m)
    q = q_s.reshape(B, L, SWA_KV_HEADS, SWA_GROUPS, SWA_HD)
    k = k_s.reshape(B, L, SWA_KV_HEADS, SWA_HD)
    v = v_s.reshape(B, L, SWA_KV_HEADS, SWA_HD)
    y_swa = swa_context(q, k, v, p["swa_sink"])
    out = merge_branches(y_rnn, y_mla, y_swa, g_rnn, g_mla, g_swa, merge_logits, p)
    x = x + gate[:, None, :] * out
    h_state = jnp.stack([h_f[:, -1], h_b[:, 0]], axis=1)
    return x, ckv, k_rope, k, v, h_state


def latent_layer(x, cond, ckv_ctx, krope_ctx, k_ctx, v_ctx, h_ctx, cos_m, sin_m, cos_s, sin_s, p):
    B, N, _ = x.shape
    gate, (x_rnn, g_rnn, c_q, c_kv, k_rope, g_mla, q_s, k_s, v_s, g_swa, merge_logits) = mix_inputs(x, cond, p)
    h_f, h_b = rglru_branch(x_rnn, h_ctx[:, 0], h_ctx[:, 1], p)
    y_rnn = h_f + h_b
    ckv = rmsnorm(c_kv, p["mla_kv_norm"])
    q_nope, q_rope = mla_q(c_q, p)
    q_rope = apply_rope(q_rope, cos_m, sin_m)
    k_rope_l = apply_rope(k_rope, cos_m, sin_m)
    k_nope_l, v_l = mla_kv_up(ckv, p)
    k_nope_c, v_c = mla_kv_up(ckv_ctx, p)
    y_mla = mla_attend(q_nope, q_rope,
                       jnp.concatenate([k_nope_c, k_nope_l], axis=1),
                       jnp.concatenate([krope_ctx, k_rope_l], axis=1),
                       jnp.concatenate([v_c, v_l], axis=1))
    q = apply_rope(q_s.reshape(B, N, SWA_KV_HEADS, SWA_GROUPS, SWA_HD), cos_s, sin_s)
    k = apply_rope(k_s.reshape(B, N, SWA_KV_HEADS, SWA_HD), cos_s, sin_s)
    v = v_s.reshape(B, N, SWA_KV_HEADS, SWA_HD)
    y_swa = swa_latent(q, k, v, k_ctx, v_ctx, p["swa_sink"])
    out = merge_branches(y_rnn, y_mla, y_swa, g_rnn, g_mla, g_swa, merge_logits, p)
    return x + gate[:, None, :] * out


def setup_inputs(seed: int = 0) -> dict:
    key = jax.random.key(seed)
    ks = iter(jax.random.split(key, 40))

    def nrm(shape, scale=1.0):
        return jax.random.normal(next(ks), shape, jnp.float32) * scale

    def gain(shape):
        return 1.0 + nrm(shape, 0.02)

    a8 = jax.random.uniform(next(ks), (DEPTH, 2, D_RNN), jnp.float32, minval=0.9, maxval=0.999)
    a = a8 ** (1.0 / LRU_C)
    lam = jnp.log(a) - jnp.log1p(-a)
    return {
        "x_prompt": nrm((BATCH, SEQ, D_MODEL)),
        "x_sample": nrm((DEC_BATCH, DEC_SEQ, D_MODEL)),
        "cache_mla_ckv": nrm((DEC_BATCH, DEPTH, PAST_LEN, KV_LORA)),
        "cache_mla_krope": nrm((DEC_BATCH, DEPTH, PAST_LEN, MLA_ROPE)),
        "cache_swa_k": nrm((DEC_BATCH, DEPTH, PAST_LEN, SWA_KV_HEADS, SWA_HD)),
        "cache_swa_v": nrm((DEC_BATCH, DEPTH, PAST_LEN, SWA_KV_HEADS, SWA_HD)),
        "state_rglru": nrm((DEC_BATCH, DEPTH, 2, D_RNN), 0.5),
        "c": nrm((DEC_BATCH, D_MODEL)),
        "c_ctx": nrm((D_MODEL,)),
        "w_mod": nrm((DEPTH, D_MODEL, 3 * D_MODEL), 0.5 * D_MODEL ** -0.5),
        "b_mod": nrm((DEPTH, 3 * D_MODEL), 0.02),
        "g_norm": gain((DEPTH, D_MODEL)),
        "w_in": nrm((DEPTH, D_MODEL, D_IN), D_MODEL ** -0.5),
        "conv_w": nrm((DEPTH, CONV_W, D_RNN), CONV_W ** -0.5),
        "conv_b": nrm((DEPTH, D_RNN), 0.02),
        "lru_wa": nrm((DEPTH, 2, RNN_BLOCKS, RNN_BW, RNN_BW), RNN_BW ** -0.5),
        "lru_ba": nrm((DEPTH, 2, D_RNN), 0.02),
        "lru_wi": nrm((DEPTH, 2, RNN_BLOCKS, RNN_BW, RNN_BW), RNN_BW ** -0.5),
        "lru_bi": nrm((DEPTH, 2, D_RNN), 0.02),
        "lru_lam": lam,
        "mla_q_norm": gain((DEPTH, Q_LORA)),
        "mla_w_uq": nrm((DEPTH, Q_LORA, MLA_HEADS * (MLA_NOPE + MLA_ROPE)), Q_LORA ** -0.5),
        "mla_kv_norm": gain((DEPTH, KV_LORA)),
        "mla_w_ukv": nrm((DEPTH, KV_LORA, MLA_HEADS * (MLA_NOPE + MLA_V)), KV_LORA ** -0.5),
        "swa_sink": nrm((DEPTH, SWA_HEADS), 0.5),
        "w_br_rnn": nrm((DEPTH, D_RNN, D_MODEL), D_RNN ** -0.5),
        "w_br_mla": nrm((DEPTH, MLA_WIDTH, D_MODEL), MLA_WIDTH ** -0.5),
        "w_br_swa": nrm((DEPTH, SWA_WIDTH, D_MODEL), SWA_WIDTH ** -0.5),
        "w_out": nrm((DEPTH, D_MODEL, D_MODEL), D_MODEL ** -0.5),
        "final_norm": gain((D_MODEL,)),
    }


def reference(x_prompt, x_sample, cache_mla_ckv, cache_mla_krope, cache_swa_k, cache_swa_v, state_rglru,
              c, c_ctx, w_mod, b_mod, g_norm, w_in, conv_w, conv_b, lru_wa, lru_ba, lru_wi, lru_bi, lru_lam,
              mla_q_norm, mla_w_uq, mla_kv_norm, mla_w_ukv, swa_sink, w_br_rnn, w_br_mla, w_br_swa, w_out,
              final_norm):
    layers = [dict(w_mod=w_mod[l], b_mod=b_mod[l], g_norm=g_norm[l], w_in=w_in[l], conv_w=conv_w[l],
                   conv_b=conv_b[l], lru_wa=lru_wa[l], lru_ba=lru_ba[l], lru_wi=lru_wi[l], lru_bi=lru_bi[l],
                   lru_lam=lru_lam[l], mla_q_norm=mla_q_norm[l], mla_w_uq=mla_w_uq[l],
                   mla_kv_norm=mla_kv_norm[l], mla_w_ukv=mla_w_ukv[l], swa_sink=swa_sink[l],
                   w_br_rnn=w_br_rnn[l], w_br_mla=w_br_mla[l], w_br_swa=w_br_swa[l], w_out=w_out[l])
              for l in range(DEPTH)]

    xp = x_prompt
    cond_p = jnp.broadcast_to(c_ctx, (x_prompt.shape[0], D_MODEL))
    ckvs, krs, sks, svs, hs = [], [], [], [], []
    for l in range(DEPTH):
        xp, ckv, kr, sk, sv, hst = context_layer(xp, cond_p, layers[l])
        ckvs.append(ckv)
        krs.append(kr)
        sks.append(sk)
        svs.append(sv)
        hs.append(hst)
    y_prompt = rmsnorm(xp, final_norm)
    new_mla_ckv = jnp.stack(ckvs, axis=1)
    new_mla_krope = jnp.stack(krs, axis=1)
    new_swa_k = jnp.stack(sks, axis=1)
    new_swa_v = jnp.stack(svs, axis=1)
    new_rglru = jnp.stack(hs, axis=1)

    n_lat = x_sample.shape[1]
    cos_m, sin_m = axial_rope_tables(n_lat, MLA_ROPE)
    cos_s, sin_s = axial_rope_tables(n_lat, SWA_HD)
    xs = x_sample
    for l in range(DEPTH):
        xs = latent_layer(xs, c, cache_mla_ckv[:, l], cache_mla_krope[:, l], cache_swa_k[:, l], cache_swa_v[:, l],
                          state_rglru[:, l], cos_m, sin_m, cos_s, sin_s, layers[l])
    y_sample = rmsnorm(xs, final_norm)
    return (y_prompt, y_sample, new_mla_ckv, new_mla_krope, new_swa_k, new_swa_v, new_rglru)
```

```python
import functools

import jax
import jax.numpy as jnp
from jax import lax
from jax.experimental import pallas as pl
from jax.experimental.pallas import tpu as pltpu

F32 = jnp.float32
BF16 = jnp.bfloat16

D_MODEL = 1024
GRID_W = 64
EPS = 1e-6
ROPE_BASE = 10000.0
NEG_INF = -1e30
D_RNN = 1024
RNN_BLOCKS = 8
RNN_BW = D_RNN // RNN_BLOCKS
LRU_C = 8.0
MLA_HEADS = 8
MLA_NOPE = 64
MLA_ROPE = 32
MLA_V = 64
MLA_HEAD_PAD = 128
Q_LORA = 384
KV_LORA = 256
MLA_WIDTH = MLA_HEADS * MLA_V
SWA_HEADS = 8
SWA_KV_HEADS = 2
SWA_GROUPS = SWA_HEADS // SWA_KV_HEADS
SWA_HD = 64
WINDOW = 128
SWA_WIDTH = SWA_HEADS * SWA_HD
N_GATES = D_RNN + MLA_WIDTH + SWA_WIDTH + 3 * D_MODEL

SUBLANES = 8
LANES = 128
VMEM_LIMIT = 56 * 1024 * 1024


def _dot(a, b):
    return jnp.dot(a, b, preferred_element_type=F32)


def _dot_nt(a, b):
    return lax.dot_general(a, b, (((1,), (1,)), ((), ())), preferred_element_type=F32)


def _rms(x, g):
    ms = jnp.mean(x * x, axis=-1, keepdims=True)
    return x * lax.rsqrt(ms + EPS) * g


def _norm_mod(x, g, mod):
    return _rms(x, g) * (1.0 + mod[1:2]) + mod[0:1]


def _swap_row_chunks(x, q):
    parts = []
    for c in range(x.shape[0] // (2 * q)):
        parts.append(x[(2 * c + 1) * q:(2 * c + 2) * q])
        parts.append(x[2 * c * q:(2 * c + 1) * q])
    return jnp.concatenate(parts, axis=0)


def _swap_lane_chunks(x, q):
    w = x.shape[-1]
    lane = lax.broadcasted_iota(jnp.int32, x.shape, x.ndim - 1)
    first = (lane % (2 * q)) < q
    return jnp.where(first, pltpu.roll(x, w - q, x.ndim - 1), pltpu.roll(x, q, x.ndim - 1))


def _mod_kernel(c_ref, w_ref, b_ref, o_ref):
    c = c_ref[...]
    a = (c * jax.nn.sigmoid(c)).astype(BF16)
    o_ref[0] = _dot(a, w_ref[0].astype(BF16)) + b_ref[0]


def _modulation(conds, w_mod, b_mod):
    depth = w_mod.shape[0]
    n = conds.shape[0]
    tn = 1024
    return pl.pallas_call(
        _mod_kernel,
        grid=(depth, 3 * D_MODEL // tn),
        in_specs=[
            pl.BlockSpec((n, D_MODEL), lambda l, j: (0, 0)),
            pl.BlockSpec((1, D_MODEL, tn), lambda l, j: (l, 0, j)),
            pl.BlockSpec((1, 1, tn), lambda l, j: (l, 0, j)),
        ],
        out_specs=pl.BlockSpec((1, n, tn), lambda l, j: (l, 0, j)),
        out_shape=jax.ShapeDtypeStruct((depth, n, 3 * D_MODEL), F32),
        compiler_params=pltpu.CompilerParams(vmem_limit_bytes=VMEM_LIMIT),
        name="modulation",
    )(conds, w_mod, b_mod.reshape(depth, 1, 3 * D_MODEL))


N_ROW_LAT = D_RNN + Q_LORA + KV_LORA + LANES + LANES
N_ROW_CTX = N_ROW_LAT + LANES
N_T = SWA_WIDTH + SWA_KV_HEADS * SWA_HD


def _proj_kernel(*refs, latent):
    if latent:
        (x_ref, mod_ref, gn_ref, wrow_ref, wt_ref, gq_ref, gkv_ref, wuqt_ref, wuk_ref, wuvt_ref,
         cq_ref, sq_ref, ck_ref, sk_ref, cs_ref, ss_ref, cks_ref, sks_ref,
         xr_ref, qt_ref, k_ref, vt_ref, qst_ref, ks_ref, vst_ref) = refs
    else:
        (x_ref, mod_ref, gn_ref, wrow_ref, wt_ref, gq_ref, gkv_ref, wuqt_ref, wuk_ref, wuvt_ref,
         xr_ref, qt_ref, k_ref, vt_ref, qst_ref, ks_ref, vst_ref,
         ckv_ref, kr_ref, ksraw_ref, vsraw_ref) = refs

    h = _norm_mod(x_ref[0], gn_ref[...], mod_ref[0]).astype(BF16)
    r = _dot(h, wrow_ref[...])
    o = 0
    xr_ref[0] = r[:, o:o + D_RNN]
    o += D_RNN
    c_q = r[:, o:o + Q_LORA]
    o += Q_LORA
    c_kv = r[:, o:o + KV_LORA]
    o += KV_LORA
    kr = r[:, o:o + LANES]
    o += LANES
    ks = r[:, o:o + LANES]
    o += LANES
    t = _dot_nt(wt_ref[...], h)
    qs_t = t[:SWA_WIDTH]
    vs_t = t[SWA_WIDTH:]

    cqn = _rms(c_q, gq_ref[...]).astype(BF16)
    q_t = _dot_nt(wuqt_ref[...], cqn)
    for hd in range(MLA_HEADS):
        base = hd * MLA_HEAD_PAD
        blk = q_t[base:base + MLA_HEAD_PAD]
        if latent:
            rp = blk[MLA_NOPE:MLA_NOPE + MLA_ROPE]
            rp = rp * cq_ref[...] + _swap_row_chunks(rp, MLA_ROPE // 4) * sq_ref[...]
            blk = jnp.concatenate([blk[:MLA_NOPE], rp, blk[MLA_NOPE + MLA_ROPE:]], axis=0)
        qt_ref[0, hd] = blk.astype(BF16)

    ckv = _rms(c_kv, gkv_ref[...])
    ckv16 = ckv.astype(BF16)
    if latent:
        kr = kr * ck_ref[...] + _swap_lane_chunks(kr, MLA_ROPE // 4) * sk_ref[...]
    kn = _dot(ckv16, wuk_ref[...])
    for hd in range(MLA_HEADS):
        k_ref[0, hd] = (kn[:, hd * MLA_HEAD_PAD:(hd + 1) * MLA_HEAD_PAD] + kr).astype(BF16)
    v_t = _dot_nt(wuvt_ref[...], ckv16)
    for hd in range(MLA_HEADS):
        vt_ref[0, hd] = v_t[hd * MLA_V:(hd + 1) * MLA_V].astype(BF16)

    if latent:
        parts = []
        for hd in range(SWA_HEADS):
            qh = qs_t[hd * SWA_HD:(hd + 1) * SWA_HD]
            parts.append(qh * cs_ref[...] + _swap_row_chunks(qh, SWA_HD // 4) * ss_ref[...])
        qs_t = jnp.concatenate(parts, axis=0)
        ks_r = ks * cks_ref[...] + _swap_lane_chunks(ks, SWA_HD // 4) * sks_ref[...]
    else:
        ks_r = ks
    qst_ref[0] = qs_t.astype(BF16)
    ks_ref[0] = ks_r.astype(BF16)
    vst_ref[0] = vs_t.astype(BF16)

    if not latent:
        ckv_ref[0] = ckv
        kr_ref[0] = kr
        ksraw_ref[0] = ks
        vsraw_ref[0] = r[:, o:o + LANES]


def _const_spec(shape):
    nd = len(shape)
    return pl.BlockSpec(shape, lambda b, i: (0,) * nd, pipeline_mode=pl.Buffered(1))


def _proj(x, mod, lw, rope, *, latent, tm):
    bsz, seq, _ = x.shape
    per_batch_mod = mod.shape[0] > 1
    n_row = N_ROW_LAT if latent else N_ROW_CTX
    tok3 = lambda w: pl.BlockSpec((1, tm, w), lambda b, i: (b, i, 0))
    feat3 = lambda f: pl.BlockSpec((1, f, tm), lambda b, i: (b, 0, i))
    in_specs = [
        tok3(D_MODEL),
        pl.BlockSpec((1, 3, D_MODEL), (lambda b, i: (b, 0, 0)) if per_batch_mod else (lambda b, i: (0, 0, 0))),
        _const_spec((1, D_MODEL)),
        _const_spec((D_MODEL, n_row)),
        _const_spec((N_T, D_MODEL)),
        _const_spec((1, Q_LORA)),
        _const_spec((1, KV_LORA)),
        _const_spec((MLA_HEADS * MLA_HEAD_PAD, Q_LORA)),
        _const_spec((KV_LORA, MLA_HEADS * MLA_HEAD_PAD)),
        _const_spec((MLA_WIDTH, KV_LORA)),
    ]
    args = [x, mod, lw["g_norm"], lw["w_row"], lw["w_t"], lw["g_q"], lw["g_kv"], lw["w_uq_t"], lw["w_uk"], lw["w_uv_t"]]
    if latent:
        in_specs += [
            pl.BlockSpec((MLA_ROPE, tm), lambda b, i: (0, i)),
            pl.BlockSpec((MLA_ROPE, tm), lambda b, i: (0, i)),
            pl.BlockSpec((tm, LANES), lambda b, i: (i, 0)),
            pl.BlockSpec((tm, LANES), lambda b, i: (i, 0)),
            pl.BlockSpec((SWA_HD, tm), lambda b, i: (0, i)),
            pl.BlockSpec((SWA_HD, tm), lambda b, i: (0, i)),
            pl.BlockSpec((tm, LANES), lambda b, i: (i, 0)),
            pl.BlockSpec((tm, LANES), lambda b, i: (i, 0)),
        ]
        args += list(rope)
    out_specs = [
        tok3(D_RNN),
        pl.BlockSpec((1, MLA_HEADS, MLA_HEAD_PAD, tm), lambda b, i: (b, 0, 0, i)),
        pl.BlockSpec((1, MLA_HEADS, tm, MLA_HEAD_PAD), lambda b, i: (b, 0, i, 0)),
        pl.BlockSpec((1, MLA_HEADS, MLA_V, tm), lambda b, i: (b, 0, 0, i)),
        feat3(SWA_WIDTH),
        tok3(LANES),
        feat3(SWA_KV_HEADS * SWA_HD),
    ]
    out_shape = [
        jax.ShapeDtypeStruct((bsz, seq, D_RNN), F32),
        jax.ShapeDtypeStruct((bsz, MLA_HEADS, MLA_HEAD_PAD, seq), BF16),
        jax.ShapeDtypeStruct((bsz, MLA_HEADS, seq, MLA_HEAD_PAD), BF16),
        jax.ShapeDtypeStruct((bsz, MLA_HEADS, MLA_V, seq), BF16),
        jax.ShapeDtypeStruct((bsz, SWA_WIDTH, seq), BF16),
        jax.ShapeDtypeStruct((bsz, seq, LANES), BF16),
        jax.ShapeDtypeStruct((bsz, SWA_KV_HEADS * SWA_HD, seq), BF16),
    ]
    if not latent:
        out_specs += [tok3(KV_LORA), tok3(LANES), tok3(LANES), tok3(LANES)]
        out_shape += [
            jax.ShapeDtypeStruct((bsz, seq, KV_LORA), F32),
            jax.ShapeDtypeStruct((bsz, seq, LANES), F32),
            jax.ShapeDtypeStruct((bsz, seq, LANES), F32),
            jax.ShapeDtypeStruct((bsz, seq, LANES), F32),
        ]
    return pl.pallas_call(
        functools.partial(_proj_kernel, latent=latent),
        grid=(bsz, seq // tm),
        in_specs=in_specs,
        out_specs=out_specs,
        out_shape=out_shape,
        compiler_params=pltpu.CompilerParams(vmem_limit_bytes=VMEM_LIMIT),
        name="proj_latent" if latent else "proj_context",
    )(*args)


def _cache_kernel(ckv_ref, kr_ref, wuk_ref, wuvt_ref, k_ref, vt_ref):
    ckv16 = ckv_ref[0].astype(BF16)
    kn = _dot(ckv16, wuk_ref[...])
    kr = kr_ref[0]
    for hd in range(MLA_HEADS):
        k_ref[0, hd] = (kn[:, hd * MLA_HEAD_PAD:(hd + 1) * MLA_HEAD_PAD] + kr).astype(BF16)
    v_t = _dot_nt(wuvt_ref[...], ckv16)
    for hd in range(MLA_HEADS):
        vt_ref[0, hd] = v_t[hd * MLA_V:(hd + 1) * MLA_V].astype(BF16)


def _cache_proj(ckv, kr_blk, lw):
    bsz, past, _ = ckv.shape
    return pl.pallas_call(
        _cache_kernel,
        grid=(bsz,),
        in_specs=[
            pl.BlockSpec((1, past, KV_LORA), lambda b: (b, 0, 0)),
            pl.BlockSpec((1, past, LANES), lambda b: (b, 0, 0)),
            pl.BlockSpec((KV_LORA, MLA_HEADS * MLA_HEAD_PAD), lambda b: (0, 0)),
            pl.BlockSpec((MLA_WIDTH, KV_LORA), lambda b: (0, 0)),
        ],
        out_specs=[
            pl.BlockSpec((1, MLA_HEADS, past, MLA_HEAD_PAD), lambda b: (b, 0, 0, 0)),
            pl.BlockSpec((1, MLA_HEADS, MLA_V, past), lambda b: (b, 0, 0, 0)),
        ],
        out_shape=[
            jax.ShapeDtypeStruct((bsz, MLA_HEADS, past, MLA_HEAD_PAD), BF16),
            jax.ShapeDtypeStruct((bsz, MLA_HEADS, MLA_V, past), BF16),
        ],
        name="cache_proj",
    )(ckv, kr_blk, lw["w_uk"], lw["w_uv_t"])


def _log_sigmoid(x):
    return jnp.minimum(x, 0.0) - jnp.log1p(jnp.exp(-jnp.abs(x)))


def _rglru_direction(d, x, prev_row, next_rows, cw_ref, cb_ref, wa_ref, ba_ref, wi_ref, bi_ref, lam_ref,
                     a_scr, u_scr, carry, out_ref):
    tl = x.shape[0]
    groups = tl // SUBLANES
    row = lax.broadcasted_iota(jnp.int32, x.shape, 0)
    xm1 = jnp.where(row == 0, prev_row, pltpu.roll(x, 1, 0))
    xp1 = jnp.where(row == tl - 1, next_rows[0:1], pltpu.roll(x, tl - 1, 0))
    xp2 = jnp.where(row == tl - 2, next_rows[0:1], jnp.where(row == tl - 1, next_rows[1:2], pltpu.roll(x, tl - 2, 0)))
    xc = cw_ref[0:1] * xm1 + cw_ref[1:2] * x + cw_ref[2:3] * xp1 + cw_ref[3:4] * xp2 + cb_ref[...]
    xcb = xc.astype(BF16)
    ra, ri = [], []
    for n in range(RNN_BLOCKS):
        blk = xcb[:, n * RNN_BW:(n + 1) * RNN_BW]
        ra.append(_dot(blk, wa_ref[d, n]))
        ri.append(_dot(blk, wi_ref[d, n]))
    r = jax.nn.sigmoid(jnp.concatenate(ra, axis=1) + ba_ref[d])
    i = jax.nn.sigmoid(jnp.concatenate(ri, axis=1) + bi_ref[d])
    log_a = (LRU_C * _log_sigmoid(lam_ref[d])) * r
    a = jnp.exp(log_a)
    u = jnp.sqrt(-jnp.tanh(log_a) * (a * a + 1.0)) * i * xc

    a3 = a.reshape(groups, SUBLANES, D_RNN)
    u3 = u.reshape(groups, SUBLANES, D_RNN)
    sub = lax.broadcasted_iota(jnp.int32, a3.shape, 1)
    for step in (1, 2, 4):
        if d == 0:
            ok = sub >= step
            shift = step
        else:
            ok = sub < SUBLANES - step
            shift = SUBLANES - step
        a_sh = jnp.where(ok, pltpu.roll(a3, shift, 1), 1.0)
        u_sh = jnp.where(ok, pltpu.roll(u3, shift, 1), 0.0)
        u3 = a3 * u_sh + u3
        a3 = a3 * a_sh
    a_scr[...] = a3
    u_scr[...] = u3

    def body(k, hp):
        g = k if d == 0 else groups - 1 - k
        hg = u_scr[g] + a_scr[g] * hp
        out_ref[0, pl.ds(pl.multiple_of(g * SUBLANES, SUBLANES), SUBLANES), :] = hg
        edge = hg[SUBLANES - 1:SUBLANES] if d == 0 else hg[0:1]
        return jnp.broadcast_to(edge, (SUBLANES, D_RNN))

    return lax.fori_loop(0, groups, body, carry)


def _rglru_kernel(xf_ref, xfp_ref, xfn_ref, xb_ref, xbp_ref, xbn_ref, cw_ref, cb_ref, wa_ref, ba_ref, wi_ref, bi_ref,
                  lam_ref, h0_ref, hf_ref, hb_ref, a_scr, u_scr, cf_scr, cbk_scr):
    i = pl.program_id(1)
    nc = pl.num_programs(1)

    @pl.when(i == 0)
    def _():
        cf_scr[...] = jnp.broadcast_to(h0_ref[0, 0:1], (SUBLANES, D_RNN))
        cbk_scr[...] = jnp.broadcast_to(h0_ref[0, 1:2], (SUBLANES, D_RNN))

    zero_row = jnp.zeros((1, D_RNN), F32)
    prev_row = jnp.where(i > 0, xfp_ref[0, SUBLANES - 1:SUBLANES], zero_row)
    next_rows = jnp.where(i < nc - 1, xfn_ref[0, 0:2], jnp.zeros((2, D_RNN), F32))
    cf_scr[...] = _rglru_direction(0, xf_ref[0], prev_row, next_rows, cw_ref, cb_ref, wa_ref, ba_ref, wi_ref, bi_ref,
                                   lam_ref, a_scr, u_scr, cf_scr[...], hf_ref)
    j = nc - 1 - i
    prev_row = jnp.where(j > 0, xbp_ref[0, SUBLANES - 1:SUBLANES], zero_row)
    next_rows = jnp.where(j < nc - 1, xbn_ref[0, 0:2], jnp.zeros((2, D_RNN), F32))
    cbk_scr[...] = _rglru_direction(1, xb_ref[0], prev_row, next_rows, cw_ref, cb_ref, wa_ref, ba_ref, wi_ref, bi_ref,
                                    lam_ref, a_scr, u_scr, cbk_scr[...], hb_ref)


def _rglru(x_rnn, h0, lw, *, tl):
    bsz, seq, _ = x_rnn.shape
    nc = seq // tl
    rows8 = seq // SUBLANES
    per = tl // SUBLANES
    main = lambda f: pl.BlockSpec((1, tl, D_RNN), lambda b, i: (b, f(i), 0))
    prev8 = lambda f: pl.BlockSpec((1, SUBLANES, D_RNN), lambda b, i: (b, jnp.maximum(f(i) * per - 1, 0), 0))
    next8 = lambda f: pl.BlockSpec((1, SUBLANES, D_RNN), lambda b, i: (b, jnp.minimum((f(i) + 1) * per, rows8 - 1), 0))
    fwd = lambda i: i
    bwd = lambda i: nc - 1 - i
    const = lambda shape: pl.BlockSpec(shape, lambda b, i: (0,) * len(shape))
    return pl.pallas_call(
        _rglru_kernel,
        grid=(bsz, nc),
        in_specs=[
            main(fwd), prev8(fwd), next8(fwd), main(bwd), prev8(bwd), next8(bwd),
            const((4, D_RNN)), const((1, D_RNN)),
            const((2, RNN_BLOCKS, RNN_BW, RNN_BW)), const((2, 1, D_RNN)),
            const((2, RNN_BLOCKS, RNN_BW, RNN_BW)), const((2, 1, D_RNN)),
            const((2, 1, D_RNN)),
            pl.BlockSpec((1, 2, D_RNN), lambda b, i: (b, 0, 0)),
        ],
        out_specs=[main(fwd), main(bwd)],
        out_shape=[jax.ShapeDtypeStruct((bsz, seq, D_RNN), F32)] * 2,
        scratch_shapes=[
            pltpu.VMEM((per, SUBLANES, D_RNN), F32),
            pltpu.VMEM((per, SUBLANES, D_RNN), F32),
            pltpu.VMEM((SUBLANES, D_RNN), F32),
            pltpu.VMEM((SUBLANES, D_RNN), F32),
        ],
        compiler_params=pltpu.CompilerParams(
            dimension_semantics=("arbitrary", "arbitrary"), vmem_limit_bytes=VMEM_LIMIT),
        name="rglru",
    )(x_rnn, x_rnn, x_rnn, x_rnn, x_rnn, x_rnn, lw["conv_w"], lw["conv_b"], lw["lru_wa"], lw["lru_ba"],
      lw["lru_wi"], lw["lru_bi"], lw["lru_lam"], h0)


MLA_SCALE = (MLA_NOPE + MLA_ROPE) ** -0.5


def _mla_kernel(*refs, has_ctx):
    if has_ctx:
        qt_ref, k_ref, vt_ref, kc_ref, vtc_ref, o_ref = refs
    else:
        qt_ref, k_ref, vt_ref, o_ref = refs
    q = qt_ref[0, 0]
    s = _dot(k_ref[0, 0], q)
    m = jnp.max(s, axis=0, keepdims=True)
    if has_ctx:
        sc = _dot(kc_ref[0, 0], q)
        m = jnp.maximum(m, jnp.max(sc, axis=0, keepdims=True))
    p = jnp.exp((s - m) * MLA_SCALE)
    l = jnp.sum(p, axis=0, keepdims=True)
    o = _dot(vt_ref[0, 0], p.astype(BF16))
    if has_ctx:
        pc = jnp.exp((sc - m) * MLA_SCALE)
        l = l + jnp.sum(pc, axis=0, keepdims=True)
        o = o + _dot(vtc_ref[0, 0], pc.astype(BF16))
    o_ref[0, 0] = o / l


def _mla(q_t, k, v_t, ctx, *, tq):
    bsz, heads, _, seq = q_t.shape
    in_specs = [
        pl.BlockSpec((1, 1, MLA_HEAD_PAD, tq), lambda b, h, i: (b, h, 0, i)),
        pl.BlockSpec((1, 1, seq, MLA_HEAD_PAD), lambda b, h, i: (b, h, 0, 0)),
        pl.BlockSpec((1, 1, MLA_V, seq), lambda b, h, i: (b, h, 0, 0)),
    ]
    args = [q_t, k, v_t]
    if ctx is not None:
        past = ctx[0].shape[2]
        in_specs += [
            pl.BlockSpec((1, 1, past, MLA_HEAD_PAD), lambda b, h, i: (b, h, 0, 0)),
            pl.BlockSpec((1, 1, MLA_V, past), lambda b, h, i: (b, h, 0, 0)),
        ]
        args += list(ctx)
    out = pl.pallas_call(
        functools.partial(_mla_kernel, has_ctx=ctx is not None),
        grid=(bsz, heads, seq // tq),
        in_specs=in_specs,
        out_specs=pl.BlockSpec((1, 1, MLA_V, tq), lambda b, h, i: (b, h, 0, i)),
        out_shape=jax.ShapeDtypeStruct((bsz, heads, MLA_V, seq), F32),
        compiler_params=pltpu.CompilerParams(vmem_limit_bytes=VMEM_LIMIT),
        name="mla_latent" if ctx is not None else "mla_context",
    )(*args)
    return out.reshape(bsz, heads * MLA_V, seq)


SWA_SCALE = SWA_HD ** -0.5


def _swa_kernel(*refs, windowed, tq, seq):
    if windowed:
        sink_ref, qt_ref, k_ref, vt_ref, kc_ref, vtc_ref, o_ref = refs
    else:
        sink_ref, qt_ref, kc_ref, vtc_ref, o_ref = refs
    i = pl.program_id(1)
    start = i * tq
    nq = SWA_GROUPS * tq
    lane = lax.broadcasted_iota(jnp.int32, (1, nq), 1)
    if windowed:
        win = tq + 2 * WINDOW
        ws = pl.multiple_of(jnp.clip(start - WINDOW, 0, seq - win), LANES)
        kpos = ws + lax.broadcasted_iota(jnp.int32, (win, 1), 0)
        qpos = start + lane % tq
        valid = jnp.abs(kpos - qpos) <= WINDOW
        kw = k_ref[0, pl.ds(ws, win), :]
    zeros = jnp.zeros((SWA_HD, nq), BF16)
    for g in range(SWA_KV_HEADS):
        qg = jnp.concatenate(
            [qt_ref[0, (g * SWA_GROUPS + j) * SWA_HD:(g * SWA_GROUPS + j + 1) * SWA_HD, :] for j in range(SWA_GROUPS)],
            axis=1)
        qp = jnp.concatenate([qg, zeros] if g == 0 else [zeros, qg], axis=0)
        sk = jnp.zeros((1, nq), F32)
        for j in range(SWA_GROUPS):
            sk = jnp.where(lane // tq == j, sink_ref[g * SWA_GROUPS + j], sk)
        sc = _dot(kc_ref[0], qp) * SWA_SCALE
        m = jnp.maximum(jnp.max(sc, axis=0, keepdims=True), sk)
        if windowed:
            s = jnp.where(valid, _dot(kw, qp) * SWA_SCALE, NEG_INF)
            m = jnp.maximum(m, jnp.max(s, axis=0, keepdims=True))
        pc = jnp.exp(sc - m)
        l = jnp.sum(pc, axis=0, keepdims=True) + jnp.exp(sk - m)
        o = _dot(vtc_ref[0, g * SWA_HD:(g + 1) * SWA_HD, :], pc.astype(BF16))
        if windowed:
            p = jnp.exp(s - m)
            l = l + jnp.sum(p, axis=0, keepdims=True)
            o = o + _dot(vt_ref[0, g * SWA_HD:(g + 1) * SWA_HD, pl.ds(ws, win)], p.astype(BF16))
        o = o / l
        for j in range(SWA_GROUPS):
            hd = g * SWA_GROUPS + j
            o_ref[0, hd * SWA_HD:(hd + 1) * SWA_HD, :] = o[:, j * tq:(j + 1) * tq]


def _swa(sink, qs_t, ks, vs_t, ctx, *, tq):
    bsz, _, seq = qs_t.shape
    windowed = ctx is not None
    whole_k = lambda n: pl.BlockSpec((1, n, LANES), lambda b, i: (b, 0, 0))
    whole_vt = lambda n: pl.BlockSpec((1, SWA_KV_HEADS * SWA_HD, n), lambda b, i: (b, 0, 0))
    in_specs = [
        pl.BlockSpec(memory_space=pltpu.SMEM),
        pl.BlockSpec((1, SWA_WIDTH, tq), lambda b, i: (b, 0, i)),
        whole_k(seq), whole_vt(seq),
    ]
    args = [sink, qs_t, ks, vs_t]
    if windowed:
        past = ctx[0].shape[1]
        in_specs += [whole_k(past), whole_vt(past)]
        args += list(ctx)
    return pl.pallas_call(
        functools.partial(_swa_kernel, windowed=windowed, tq=tq, seq=seq),
        grid=(bsz, seq // tq),
        in_specs=in_specs,
        out_specs=pl.BlockSpec((1, SWA_WIDTH, tq), lambda b, i: (b, 0, i)),
        out_shape=jax.ShapeDtypeStruct((bsz, SWA_WIDTH, seq), F32),
        compiler_params=pltpu.CompilerParams(vmem_limit_bytes=VMEM_LIMIT),
        name="swa_latent" if windowed else "swa_context",
    )(*args)


def _silu(x):
    return x * jax.nn.sigmoid(x)


def _merge_kernel(*refs, final):
    if final:
        (x_ref, mod_ref, gn_ref, hf_ref, hb_ref, ym_ref, ys_ref, wg_ref, wr_ref, wm_ref, ws_ref, wo_ref, fn_ref,
         o_ref) = refs
    else:
        (x_ref, mod_ref, gn_ref, hf_ref, hb_ref, ym_ref, ys_ref, wg_ref, wr_ref, wm_ref, ws_ref, wo_ref,
         o_ref) = refs
    x = x_ref[0]
    mod = mod_ref[0]
    h = _norm_mod(x, gn_ref[...], mod).astype(BF16)
    g = _dot(h, wg_ref[...])
    o = 0
    g_rnn = g[:, o:o + D_RNN]
    o += D_RNN
    g_mla = g[:, o:o + MLA_WIDTH]
    o += MLA_WIDTH
    g_swa = g[:, o:o + SWA_WIDTH]
    o += SWA_WIDTH
    y_rnn = hf_ref[0] + hb_ref[0]
    y_mla = ym_ref[0].T
    y_swa = ys_ref[0].T
    b_rnn = _dot((y_rnn * _silu(g_rnn)).astype(BF16), wr_ref[...])
    b_mla = _dot((y_mla * _silu(g_mla)).astype(BF16), wm_ref[...])
    b_swa = _dot((y_swa * _silu(g_swa)).astype(BF16), ws_ref[...])
    u = (jax.nn.sigmoid(g[:, o:o + D_MODEL]) * b_rnn
         + jax.nn.sigmoid(g[:, o + D_MODEL:o + 2 * D_MODEL]) * b_mla
         + jax.nn.sigmoid(g[:, o + 2 * D_MODEL:o + 3 * D_MODEL]) * b_swa)
    out = _dot(u.astype(BF16), wo_ref[...])
    xn = x + mod[2:3] * out
    if final:
        xn = _rms(xn, fn_ref[...])
    o_ref[0] = xn


def _merge(x, mod, lw, h_f, h_b, ym_t, ys_t, final_norm, *, tm):
    bsz, seq, _ = x.shape
    per_batch_mod = mod.shape[0] > 1
    final = final_norm is not None
    tok3 = lambda w: pl.BlockSpec((1, tm, w), lambda b, i: (b, i, 0))
    feat3 = lambda f: pl.BlockSpec((1, f, tm), lambda b, i: (b, 0, i))
    in_specs = [
        tok3(D_MODEL),
        pl.BlockSpec((1, 3, D_MODEL), (lambda b, i: (b, 0, 0)) if per_batch_mod else (lambda b, i: (0, 0, 0))),
        _const_spec((1, D_MODEL)),
        tok3(D_RNN), tok3(D_RNN), feat3(MLA_WIDTH), feat3(SWA_WIDTH),
        _const_spec((D_MODEL, N_GATES)),
        _const_spec((D_RNN, D_MODEL)),
        _const_spec((MLA_WIDTH, D_MODEL)),
        _const_spec((SWA_WIDTH, D_MODEL)),
        _const_spec((D_MODEL, D_MODEL)),
    ]
    args = [x, mod, lw["g_norm"], h_f, h_b, ym_t, ys_t, lw["w_gates"], lw["w_br_rnn"], lw["w_br_mla"], lw["w_br_swa"],
            lw["w_out"]]
    if final:
        in_specs.append(_const_spec((1, D_MODEL)))
        args.append(final_norm)
    return pl.pallas_call(
        functools.partial(_merge_kernel, final=final),
        grid=(bsz, seq // tm),
        in_specs=in_specs,
        out_specs=tok3(D_MODEL),
        out_shape=jax.ShapeDtypeStruct((bsz, seq, D_MODEL), F32),
        compiler_params=pltpu.CompilerParams(vmem_limit_bytes=VMEM_LIMIT),
        name="merge_final" if final else "merge",
    )(*args)


def _split_points():
    sizes = (D_RNN, D_RNN, Q_LORA, KV_LORA, MLA_ROPE, MLA_WIDTH, SWA_WIDTH, SWA_KV_HEADS * SWA_HD,
             SWA_KV_HEADS * SWA_HD, SWA_WIDTH, 3 * D_MODEL)
    pts, acc = [0], 0
    for s in sizes:
        acc += s
        pts.append(acc)
    return pts


def _layer_weights(l, g_norm, w_in, conv_w, conv_b, lru_wa, lru_ba, lru_wi, lru_bi, lru_lam, mla_q_norm, mla_w_uq,
                   mla_kv_norm, mla_w_ukv, w_br_rnn, w_br_mla, w_br_swa, w_out):
    p = _split_points()
    w = w_in[l]
    seg = lambda k: w[:, p[k]:p[k + 1]]
    (w_xr, w_grnn, w_cq, w_ckv, w_kr, w_gmla, w_qs, w_ks, w_vs, w_gswa, w_ml) = [seg(k) for k in range(11)]
    w_kr_blk = jnp.pad(w_kr, ((0, 0), (MLA_NOPE, MLA_HEAD_PAD - MLA_NOPE - MLA_ROPE)))
    w_row_lat = jnp.concatenate([w_xr, w_cq, w_ckv, w_kr_blk, w_ks], axis=1).astype(BF16)
    w_row_ctx = jnp.concatenate([w_xr, w_cq, w_ckv, w_kr_blk, w_ks, w_vs], axis=1).astype(BF16)
    w_t = jnp.concatenate([w_qs, w_vs], axis=1).T.astype(BF16)
    w_gates = jnp.concatenate([w_grnn, w_gmla, w_gswa, w_ml], axis=1).astype(BF16)
    uq = mla_w_uq[l].reshape(Q_LORA, MLA_HEADS, MLA_NOPE + MLA_ROPE)
    uq = jnp.pad(uq, ((0, 0), (0, 0), (0, MLA_HEAD_PAD - MLA_NOPE - MLA_ROPE)))
    w_uq_t = uq.reshape(Q_LORA, MLA_HEADS * MLA_HEAD_PAD).T.astype(BF16)
    ukv = mla_w_ukv[l].reshape(KV_LORA, MLA_HEADS, MLA_NOPE + MLA_V)
    uk = jnp.pad(ukv[:, :, :MLA_NOPE], ((0, 0), (0, 0), (0, MLA_HEAD_PAD - MLA_NOPE)))
    w_uk = uk.reshape(KV_LORA, MLA_HEADS * MLA_HEAD_PAD).astype(BF16)
    w_uv_t = ukv[:, :, MLA_NOPE:].reshape(KV_LORA, MLA_WIDTH).T.astype(BF16)
    return dict(
        g_norm=g_norm[l].reshape(1, D_MODEL), w_row_lat=w_row_lat, w_row_ctx=w_row_ctx, w_t=w_t, w_gates=w_gates,
        g_q=mla_q_norm[l].reshape(1, Q_LORA), g_kv=mla_kv_norm[l].reshape(1, KV_LORA),
        w_uq_t=w_uq_t, w_uk=w_uk, w_uv_t=w_uv_t,
        conv_w=conv_w[l], conv_b=conv_b[l].reshape(1, D_RNN),
        lru_wa=lru_wa[l].astype(BF16), lru_ba=lru_ba[l].reshape(2, 1, D_RNN),
        lru_wi=lru_wi[l].astype(BF16), lru_bi=lru_bi[l].reshape(2, 1, D_RNN),
        lru_lam=lru_lam[l].reshape(2, 1, D_RNN),
        w_br_rnn=w_br_rnn[l].astype(BF16), w_br_mla=w_br_mla[l].astype(BF16), w_br_swa=w_br_swa[l].astype(BF16),
        w_out=w_out[l].astype(BF16),
    )


def _rope_tables(n_tok, rot_dim):
    rows = n_tok // GRID_W
    row = jnp.repeat(jnp.arange(rows, dtype=F32), GRID_W)
    col = jnp.tile(jnp.arange(GRID_W, dtype=F32), rows)
    half = rot_dim // 2
    freqs = ROPE_BASE ** (-jnp.arange(0, half, 2, dtype=F32) / half)
    ang_r = row[:, None] * freqs[None, :]
    ang_c = col[:, None] * freqs[None, :]
    ang = jnp.concatenate([ang_r, ang_r, ang_c, ang_c], axis=-1)
    q = rot_dim // 4
    sign = jnp.concatenate([-jnp.ones((q,), F32), jnp.ones((q,), F32)] * 2)
    return jnp.cos(ang), jnp.sin(ang) * sign


def _rope_inputs(n_tok):
    cos_m, sin_m = _rope_tables(n_tok, MLA_ROPE)
    cos_s, sin_s = _rope_tables(n_tok, SWA_HD)
    pad_m = ((0, 0), (MLA_NOPE, MLA_HEAD_PAD - MLA_NOPE - MLA_ROPE))
    return (cos_m.T, sin_m.T, jnp.pad(cos_m, pad_m, constant_values=1.0), jnp.pad(sin_m, pad_m),
            cos_s.T, sin_s.T, jnp.tile(cos_s, (1, SWA_KV_HEADS)), jnp.tile(sin_s, (1, SWA_KV_HEADS)))


def _tile(n, pref):
    return pref if n % pref == 0 else n


def _context_layer(x, mod, lw, sink, final_norm):
    bsz, seq, _ = x.shape
    lw = dict(lw, w_row=lw["w_row_ctx"])
    (x_rnn, q_t, k, v_t, qs_t, ks, vs_t, ckv, kr_blk, ks_raw, vs_raw) = _proj(
        x, mod, lw, None, latent=False, tm=_tile(seq, 256))
    h_f, h_b = _rglru(x_rnn, jnp.zeros((bsz, 2, D_RNN), F32), lw, tl=_tile(seq, 256))
    ym_t = _mla(q_t, k, v_t, None, tq=_tile(seq, 256))
    ys_t = _swa(sink, qs_t, ks, vs_t, None, tq=_tile(seq, 256))
    x = _merge(x, mod, lw, h_f, h_b, ym_t, ys_t, final_norm, tm=_tile(seq, 256))
    h_state = jnp.stack([h_f[:, -1], h_b[:, 0]], axis=1)
    k_rope = kr_blk[:, :, MLA_NOPE:MLA_NOPE + MLA_ROPE]
    return (x, ckv, k_rope, ks_raw.reshape(bsz, seq, SWA_KV_HEADS, SWA_HD),
            vs_raw.reshape(bsz, seq, SWA_KV_HEADS, SWA_HD), h_state)


def _latent_layer(x, mod, lw, sink, rope, ckv_ctx, krope_ctx, k_ctx, v_ctx, h_ctx, final_norm):
    bsz, seq, _ = x.shape
    past = ckv_ctx.shape[1]
    lw = dict(lw, w_row=lw["w_row_lat"])
    x_rnn, q_t, k, v_t, qs_t, ks, vs_t = _proj(x, mod, lw, rope, latent=True, tm=_tile(seq, 512))
    kr_blk = jnp.pad(krope_ctx, ((0, 0), (0, 0), (MLA_NOPE, MLA_HEAD_PAD - MLA_NOPE - MLA_ROPE)))
    kc, vc_t = _cache_proj(ckv_ctx, kr_blk, lw)
    h_f, h_b = _rglru(x_rnn, h_ctx, lw, tl=_tile(seq, 256))
    ym_t = _mla(q_t, k, v_t, (kc, vc_t), tq=_tile(seq, 512))
    ks_c = k_ctx.reshape(bsz, past, SWA_KV_HEADS * SWA_HD).astype(BF16)
    vs_c_t = jnp.swapaxes(v_ctx.reshape(bsz, past, SWA_KV_HEADS * SWA_HD), 1, 2).astype(BF16)
    ys_t = _swa(sink, qs_t, ks, vs_t, (ks_c, vs_c_t), tq=_tile(seq, 256))
    return _merge(x, mod, lw, h_f, h_b, ym_t, ys_t, final_norm, tm=_tile(seq, 256))


def kernel(x_prompt, x_sample, cache_mla_ckv, cache_mla_krope, cache_swa_k, cache_swa_v, state_rglru, c, c_ctx, w_mod, b_mod, g_norm, w_in, conv_w, conv_b, lru_wa, lru_ba, lru_wi, lru_bi, lru_lam, mla_q_norm, mla_w_uq, mla_kv_norm, mla_w_ukv, swa_sink, w_br_rnn, w_br_mla, w_br_swa, w_out, final_norm):
    depth = w_in.shape[0]
    dec_b = x_sample.shape[0]
    n_cond = -(-(dec_b + 1) // SUBLANES) * SUBLANES
    conds = jnp.concatenate([c, c_ctx[None, :], jnp.zeros((n_cond - dec_b - 1, D_MODEL), F32)], axis=0)
    mod = _modulation(conds, w_mod, b_mod).reshape(depth, n_cond, 3, D_MODEL)
    fn = final_norm.reshape(1, D_MODEL)
    layers = [_layer_weights(l, g_norm, w_in, conv_w, conv_b, lru_wa, lru_ba, lru_wi, lru_bi, lru_lam, mla_q_norm,
                             mla_w_uq, mla_kv_norm, mla_w_ukv, w_br_rnn, w_br_mla, w_br_swa, w_out)
              for l in range(depth)]

    xp = x_prompt
    ckvs, krs, sks, svs, hs = [], [], [], [], []
    for l in range(depth):
        xp, ckv, kr, sk, sv, hst = _context_layer(xp, mod[l, dec_b:dec_b + 1], layers[l], swa_sink[l],
                                                  fn if l == depth - 1 else None)
        ckvs.append(ckv)
        krs.append(kr)
        sks.append(sk)
        svs.append(sv)
        hs.append(hst)

    rope = _rope_inputs(x_sample.shape[1])
    xs = x_sample
    for l in range(depth):
        xs = _latent_layer(xs, mod[l, :dec_b], layers[l], swa_sink[l], rope, cache_mla_ckv[:, l],
                           cache_mla_krope[:, l], cache_swa_k[:, l], cache_swa_v[:, l], state_rglru[:, l],
                           fn if l == depth - 1 else None)
    return (xp, xs, jnp.stack(ckvs, axis=1), jnp.stack(krs, axis=1), jnp.stack(sks, axis=1),
            jnp.stack(svs, axis=1), jnp.stack(hs, axis=1))
```

```python
import functools

import jax
import jax.numpy as jnp
from jax import lax
from jax.experimental import pallas as pl
from jax.experimental.pallas import tpu as pltpu

F32 = jnp.float32
BF16 = jnp.bfloat16

D_MODEL = 1024
GRID_W = 64
EPS = 1e-6
ROPE_BASE = 10000.0
NEG_INF = -1e30
D_RNN = 1024
RNN_BLOCKS = 8
RNN_BW = D_RNN // RNN_BLOCKS
LRU_C = 8.0
MLA_HEADS = 8
MLA_NOPE = 64
MLA_ROPE = 32
MLA_V = 64
MLA_HEAD_PAD = 128
Q_LORA = 384
KV_LORA = 256
MLA_WIDTH = MLA_HEADS * MLA_V
SWA_HEADS = 8
SWA_KV_HEADS = 2
SWA_GROUPS = SWA_HEADS // SWA_KV_HEADS
SWA_HD = 64
WINDOW = 128
SWA_WIDTH = SWA_HEADS * SWA_HD
N_GATES = D_RNN + MLA_WIDTH + SWA_WIDTH + 3 * D_MODEL
LOG2E = 1.4426950408889634
MLA_QK_SCALE = (MLA_NOPE + MLA_ROPE) ** -0.5 * LOG2E
SWA_QK_SCALE = SWA_HD ** -0.5 * LOG2E
ONES_ROWS = 16
V_ROWS = MLA_V + ONES_ROWS

SUBLANES = 8
LANES = 128
VMEM_LIMIT = 56 * 1024 * 1024


def _dot(a, b):
    return jnp.dot(a, b, preferred_element_type=F32)


def _dot_nt(a, b):
    return lax.dot_general(a, b, (((1,), (1,)), ((), ())), preferred_element_type=F32)


def _rms(x, g):
    ms = jnp.mean(x * x, axis=-1, keepdims=True)
    return x * lax.rsqrt(ms + EPS) * g


def _norm_mod(x, g, mod):
    return _rms(x, g) * (1.0 + mod[1:2]) + mod[0:1]


def _swap_row_chunks(x, q):
    parts = []
    for c in range(x.shape[0] // (2 * q)):
        parts.append(x[(2 * c + 1) * q:(2 * c + 2) * q])
        parts.append(x[2 * c * q:(2 * c + 1) * q])
    return jnp.concatenate(parts, axis=0)


def _swap_lane_chunks(x, q):
    w = x.shape[-1]
    lane = lax.broadcasted_iota(jnp.int32, x.shape, x.ndim - 1)
    first = (lane % (2 * q)) < q
    return jnp.where(first, pltpu.roll(x, w - q, x.ndim - 1), pltpu.roll(x, q, x.ndim - 1))


def _mod_kernel(c_ref, w_ref, b_ref, o_ref):
    c = c_ref[...]
    a = (c * jax.nn.sigmoid(c)).astype(BF16)
    o_ref[0] = _dot(a, w_ref[0].astype(BF16)) + b_ref[0]


def _modulation(conds, w_mod, b_mod):
    depth = w_mod.shape[0]
    n = conds.shape[0]
    tn = 1024
    return pl.pallas_call(
        _mod_kernel,
        grid=(depth, 3 * D_MODEL // tn),
        in_specs=[
            pl.BlockSpec((n, D_MODEL), lambda l, j: (0, 0)),
            pl.BlockSpec((1, D_MODEL, tn), lambda l, j: (l, 0, j)),
            pl.BlockSpec((1, 1, tn), lambda l, j: (l, 0, j)),
        ],
        out_specs=pl.BlockSpec((1, n, tn), lambda l, j: (l, 0, j)),
        out_shape=jax.ShapeDtypeStruct((depth, n, 3 * D_MODEL), F32),
        compiler_params=pltpu.CompilerParams(vmem_limit_bytes=VMEM_LIMIT),
        name="modulation",
    )(conds, w_mod, b_mod.reshape(depth, 1, 3 * D_MODEL))


N_ROW_LAT = D_RNN + Q_LORA + KV_LORA + LANES + LANES
N_ROW_CTX = N_ROW_LAT + LANES
N_T = SWA_WIDTH + SWA_KV_HEADS * SWA_HD


def _proj_kernel(*refs, latent):
    if latent:
        (x_ref, mod_ref, gn_ref, wrow_ref, wt_ref, gq_ref, gkv_ref, wuqt_ref, wuk_ref, wuvt_ref,
         cq_ref, sq_ref, ck_ref, sk_ref, cs_ref, ss_ref, cks_ref, sks_ref,
         xr_ref, qt_ref, k_ref, vt_ref, qst_ref, ks_ref, vst_ref) = refs
    else:
        (x_ref, mod_ref, gn_ref, wrow_ref, wt_ref, gq_ref, gkv_ref, wuqt_ref, wuk_ref, wuvt_ref,
         xr_ref, qt_ref, k_ref, vt_ref, qst_ref, ks_ref, vst_ref,
         ckv_ref, kr_ref, ksraw_ref, vsraw_ref) = refs

    h = _norm_mod(x_ref[0], gn_ref[...], mod_ref[0]).astype(BF16)
    r = _dot(h, wrow_ref[...])
    o = 0
    xr_ref[0] = r[:, o:o + D_RNN]
    o += D_RNN
    c_q = r[:, o:o + Q_LORA]
    o += Q_LORA
    c_kv = r[:, o:o + KV_LORA]
    o += KV_LORA
    kr = r[:, o:o + LANES]
    o += LANES
    ks = r[:, o:o + LANES]
    o += LANES
    t = _dot_nt(wt_ref[...], h)
    qs_t = t[:SWA_WIDTH] * SWA_QK_SCALE
    vs_t = t[SWA_WIDTH:]
    ones = jnp.ones((ONES_ROWS, x_ref.shape[1]), BF16)

    cqn = _rms(c_q, gq_ref[...]).astype(BF16)
    q_t = _dot_nt(wuqt_ref[...], cqn) * MLA_QK_SCALE
    for hd in range(MLA_HEADS):
        base = hd * MLA_HEAD_PAD
        blk = q_t[base:base + MLA_HEAD_PAD]
        if latent:
            rp = blk[MLA_NOPE:MLA_NOPE + MLA_ROPE]
            rp = rp * cq_ref[...] + _swap_row_chunks(rp, MLA_ROPE // 4) * sq_ref[...]
            blk = jnp.concatenate([blk[:MLA_NOPE], rp, blk[MLA_NOPE + MLA_ROPE:]], axis=0)
        qt_ref[0, hd] = blk.astype(BF16)

    ckv = _rms(c_kv, gkv_ref[...])
    ckv16 = ckv.astype(BF16)
    if latent:
        kr = kr * ck_ref[...] + _swap_lane_chunks(kr, MLA_ROPE // 4) * sk_ref[...]
    kn = _dot(ckv16, wuk_ref[...])
    for hd in range(MLA_HEADS):
        k_ref[0, hd] = (kn[:, hd * MLA_HEAD_PAD:(hd + 1) * MLA_HEAD_PAD] + kr).astype(BF16)
    v_t = _dot_nt(wuvt_ref[...], ckv16)
    for hd in range(MLA_HEADS):
        vt_ref[0, hd] = jnp.concatenate([v_t[hd * MLA_V:(hd + 1) * MLA_V].astype(BF16), ones], axis=0)

    if latent:
        parts = []
        for hd in range(SWA_HEADS):
            qh = qs_t[hd * SWA_HD:(hd + 1) * SWA_HD]
            parts.append(qh * cs_ref[...] + _swap_row_chunks(qh, SWA_HD // 4) * ss_ref[...])
        qs_t = jnp.concatenate(parts, axis=0)
        ks_r = ks * cks_ref[...] + _swap_lane_chunks(ks, SWA_HD // 4) * sks_ref[...]
    else:
        ks_r = ks
    qst_ref[0] = qs_t.astype(BF16)
    ks_ref[0] = ks_r.astype(BF16)
    for g in range(SWA_KV_HEADS):
        vst_ref[0, g] = jnp.concatenate([vs_t[g * SWA_HD:(g + 1) * SWA_HD].astype(BF16), ones], axis=0)

    if not latent:
        ckv_ref[0] = ckv
        kr_ref[0] = kr
        ksraw_ref[0] = ks
        vsraw_ref[0] = r[:, o:o + LANES]


def _const_spec(shape):
    nd = len(shape)
    return pl.BlockSpec(shape, lambda b, i: (0,) * nd, pipeline_mode=pl.Buffered(1))


def _proj(x, mod, lw, rope, *, latent, tm):
    bsz, seq, _ = x.shape
    per_batch_mod = mod.shape[0] > 1
    n_row = N_ROW_LAT if latent else N_ROW_CTX
    tok3 = lambda w: pl.BlockSpec((1, tm, w), lambda b, i: (b, i, 0))
    feat3 = lambda f: pl.BlockSpec((1, f, tm), lambda b, i: (b, 0, i))
    in_specs = [
        tok3(D_MODEL),
        pl.BlockSpec((1, 3, D_MODEL), (lambda b, i: (b, 0, 0)) if per_batch_mod else (lambda b, i: (0, 0, 0))),
        _const_spec((1, D_MODEL)),
        _const_spec((D_MODEL, n_row)),
        _const_spec((N_T, D_MODEL)),
        _const_spec((1, Q_LORA)),
        _const_spec((1, KV_LORA)),
        _const_spec((MLA_HEADS * MLA_HEAD_PAD, Q_LORA)),
        _const_spec((KV_LORA, MLA_HEADS * MLA_HEAD_PAD)),
        _const_spec((MLA_WIDTH, KV_LORA)),
    ]
    args = [x, mod, lw["g_norm"], lw["w_row"], lw["w_t"], lw["g_q"], lw["g_kv"], lw["w_uq_t"], lw["w_uk"], lw["w_uv_t"]]
    if latent:
        in_specs += [
            pl.BlockSpec((MLA_ROPE, tm), lambda b, i: (0, i)),
            pl.BlockSpec((MLA_ROPE, tm), lambda b, i: (0, i)),
            pl.BlockSpec((tm, LANES), lambda b, i: (i, 0)),
            pl.BlockSpec((tm, LANES), lambda b, i: (i, 0)),
            pl.BlockSpec((SWA_HD, tm), lambda b, i: (0, i)),
            pl.BlockSpec((SWA_HD, tm), lambda b, i: (0, i)),
            pl.BlockSpec((tm, LANES), lambda b, i: (i, 0)),
            pl.BlockSpec((tm, LANES), lambda b, i: (i, 0)),
        ]
        args += list(rope)
    out_specs = [
        tok3(D_RNN),
        pl.BlockSpec((1, MLA_HEADS, MLA_HEAD_PAD, tm), lambda b, i: (b, 0, 0, i)),
        pl.BlockSpec((1, MLA_HEADS, tm, MLA_HEAD_PAD), lambda b, i: (b, 0, i, 0)),
        pl.BlockSpec((1, MLA_HEADS, V_ROWS, tm), lambda b, i: (b, 0, 0, i)),
        feat3(SWA_WIDTH),
        tok3(LANES),
        pl.BlockSpec((1, SWA_KV_HEADS, V_ROWS, tm), lambda b, i: (b, 0, 0, i)),
    ]
    out_shape = [
        jax.ShapeDtypeStruct((bsz, seq, D_RNN), F32),
        jax.ShapeDtypeStruct((bsz, MLA_HEADS, MLA_HEAD_PAD, seq), BF16),
        jax.ShapeDtypeStruct((bsz, MLA_HEADS, seq, MLA_HEAD_PAD), BF16),
        jax.ShapeDtypeStruct((bsz, MLA_HEADS, V_ROWS, seq), BF16),
        jax.ShapeDtypeStruct((bsz, SWA_WIDTH, seq), BF16),
        jax.ShapeDtypeStruct((bsz, seq, LANES), BF16),
        jax.ShapeDtypeStruct((bsz, SWA_KV_HEADS, V_ROWS, seq), BF16),
    ]
    if not latent:
        out_specs += [tok3(KV_LORA), tok3(LANES), tok3(LANES), tok3(LANES)]
        out_shape += [
            jax.ShapeDtypeStruct((bsz, seq, KV_LORA), F32),
            jax.ShapeDtypeStruct((bsz, seq, LANES), F32),
            jax.ShapeDtypeStruct((bsz, seq, LANES), F32),
            jax.ShapeDtypeStruct((bsz, seq, LANES), F32),
        ]
    return pl.pallas_call(
        functools.partial(_proj_kernel, latent=latent),
        grid=(bsz, seq // tm),
        in_specs=in_specs,
        out_specs=out_specs,
        out_shape=out_shape,
        compiler_params=pltpu.CompilerParams(vmem_limit_bytes=VMEM_LIMIT),
        name="proj_latent" if latent else "proj_context",
    )(*args)


def _cache_kernel(ckv_ref, kr_ref, wuk_ref, wuvt_ref, k_ref, vt_ref):
    ckv16 = ckv_ref[0].astype(BF16)
    kn = _dot(ckv16, wuk_ref[...])
    kr = kr_ref[0]
    for hd in range(MLA_HEADS):
        k_ref[0, hd] = (kn[:, hd * MLA_HEAD_PAD:(hd + 1) * MLA_HEAD_PAD] + kr).astype(BF16)
    v_t = _dot_nt(wuvt_ref[...], ckv16)
    ones = jnp.ones((ONES_ROWS, ckv_ref.shape[1]), BF16)
    for hd in range(MLA_HEADS):
        vt_ref[0, hd] = jnp.concatenate([v_t[hd * MLA_V:(hd + 1) * MLA_V].astype(BF16), ones], axis=0)


def _cache_proj(ckv, kr_blk, lw):
    bsz, past, _ = ckv.shape
    return pl.pallas_call(
        _cache_kernel,
        grid=(bsz,),
        in_specs=[
            pl.BlockSpec((1, past, KV_LORA), lambda b: (b, 0, 0)),
            pl.BlockSpec((1, past, LANES), lambda b: (b, 0, 0)),
            pl.BlockSpec((KV_LORA, MLA_HEADS * MLA_HEAD_PAD), lambda b: (0, 0)),
            pl.BlockSpec((MLA_WIDTH, KV_LORA), lambda b: (0, 0)),
        ],
        out_specs=[
            pl.BlockSpec((1, MLA_HEADS, past, MLA_HEAD_PAD), lambda b: (b, 0, 0, 0)),
            pl.BlockSpec((1, MLA_HEADS, V_ROWS, past), lambda b: (b, 0, 0, 0)),
        ],
        out_shape=[
            jax.ShapeDtypeStruct((bsz, MLA_HEADS, past, MLA_HEAD_PAD), BF16),
            jax.ShapeDtypeStruct((bsz, MLA_HEADS, V_ROWS, past), BF16),
        ],
        name="cache_proj",
    )(ckv, kr_blk, lw["w_uk"], lw["w_uv_t"])


def _log_sigmoid(x):
    return jnp.minimum(x, 0.0) - jnp.log1p(jnp.exp(-jnp.abs(x)))


def _rglru_direction(d, x, prev_row, next_rows, cw_ref, cb_ref, wa_ref, ba_ref, wi_ref, bi_ref, lam_ref,
                     a_scr, u_scr, carry, out_ref):
    tl = x.shape[0]
    groups = tl // SUBLANES
    row = lax.broadcasted_iota(jnp.int32, x.shape, 0)
    xm1 = jnp.where(row == 0, prev_row, pltpu.roll(x, 1, 0))
    xp1 = jnp.where(row == tl - 1, next_rows[0:1], pltpu.roll(x, tl - 1, 0))
    xp2 = jnp.where(row == tl - 2, next_rows[0:1], jnp.where(row == tl - 1, next_rows[1:2], pltpu.roll(x, tl - 2, 0)))
    xc = cw_ref[0:1] * xm1 + cw_ref[1:2] * x + cw_ref[2:3] * xp1 + cw_ref[3:4] * xp2 + cb_ref[...]
    xcb = xc.astype(BF16)
    ra, ri = [], []
    for n in range(RNN_BLOCKS):
        blk = xcb[:, n * RNN_BW:(n + 1) * RNN_BW]
        ra.append(_dot(blk, wa_ref[d, n]))
        ri.append(_dot(blk, wi_ref[d, n]))
    r = jax.nn.sigmoid(jnp.concatenate(ra, axis=1) + ba_ref[d])
    i = jax.nn.sigmoid(jnp.concatenate(ri, axis=1) + bi_ref[d])
    a = jnp.exp2((LRU_C * LOG2E * _log_sigmoid(lam_ref[d])) * r)
    u = jnp.sqrt(jnp.maximum(1.0 - a * a, 0.0)) * i * xc

    a3 = a.reshape(groups, SUBLANES, D_RNN)
    u3 = u.reshape(groups, SUBLANES, D_RNN)
    sub = lax.broadcasted_iota(jnp.int32, a3.shape, 1)
    for step in (1, 2, 4):
        if d == 0:
            ok = sub >= step
            shift = step
        else:
            ok = sub < SUBLANES - step
            shift = SUBLANES - step
        a_sh = jnp.where(ok, pltpu.roll(a3, shift, 1), 1.0)
        u_sh = jnp.where(ok, pltpu.roll(u3, shift, 1), 0.0)
        u3 = a3 * u_sh + u3
        a3 = a3 * a_sh
    a_scr[...] = a3
    u_scr[...] = u3

    def body(k, hp):
        g = k if d == 0 else groups - 1 - k
        hg = u_scr[g] + a_scr[g] * hp
        out_ref[0, pl.ds(pl.multiple_of(g * SUBLANES, SUBLANES), SUBLANES), :] = hg
        edge = hg[SUBLANES - 1:SUBLANES] if d == 0 else hg[0:1]
        return jnp.broadcast_to(edge, (SUBLANES, D_RNN))

    return lax.fori_loop(0, groups, body, carry)


def _rglru_kernel(xf_ref, xfp_ref, xfn_ref, xb_ref, xbp_ref, xbn_ref, cw_ref, cb_ref, wa_ref, ba_ref, wi_ref, bi_ref,
                  lam_ref, h0_ref, hf_ref, hb_ref, a_scr, u_scr, cf_scr, cbk_scr):
    i = pl.program_id(1)
    nc = pl.num_programs(1)

    @pl.when(i == 0)
    def _():
        cf_scr[...] = jnp.broadcast_to(h0_ref[0, 0:1], (SUBLANES, D_RNN))
        cbk_scr[...] = jnp.broadcast_to(h0_ref[0, 1:2], (SUBLANES, D_RNN))

    zero_row = jnp.zeros((1, D_RNN), F32)
    prev_row = jnp.where(i > 0, xfp_ref[0, SUBLANES - 1:SUBLANES], zero_row)
    next_rows = jnp.where(i < nc - 1, xfn_ref[0, 0:2], jnp.zeros((2, D_RNN), F32))
    cf_scr[...] = _rglru_direction(0, xf_ref[0], prev_row, next_rows, cw_ref, cb_ref, wa_ref, ba_ref, wi_ref, bi_ref,
                                   lam_ref, a_scr, u_scr, cf_scr[...], hf_ref)
    j = nc - 1 - i
    prev_row = jnp.where(j > 0, xbp_ref[0, SUBLANES - 1:SUBLANES], zero_row)
    next_rows = jnp.where(j < nc - 1, xbn_ref[0, 0:2], jnp.zeros((2, D_RNN), F32))
    cbk_scr[...] = _rglru_direction(1, xb_ref[0], prev_row, next_rows, cw_ref, cb_ref, wa_ref, ba_ref, wi_ref, bi_ref,
                                    lam_ref, a_scr, u_scr, cbk_scr[...], hb_ref)


def _rglru(x_rnn, h0, lw, *, tl):
    bsz, seq, _ = x_rnn.shape
    nc = seq // tl
    rows8 = seq // SUBLANES
    per = tl // SUBLANES
    main = lambda f: pl.BlockSpec((1, tl, D_RNN), lambda b, i: (b, f(i), 0))
    prev8 = lambda f: pl.BlockSpec((1, SUBLANES, D_RNN), lambda b, i: (b, jnp.maximum(f(i) * per - 1, 0), 0))
    next8 = lambda f: pl.BlockSpec((1, SUBLANES, D_RNN), lambda b, i: (b, jnp.minimum((f(i) + 1) * per, rows8 - 1), 0))
    fwd = lambda i: i
    bwd = lambda i: nc - 1 - i
    const = lambda shape: pl.BlockSpec(shape, lambda b, i: (0,) * len(shape))
    return pl.pallas_call(
        _rglru_kernel,
        grid=(bsz, nc),
        in_specs=[
            main(fwd), prev8(fwd), next8(fwd), main(bwd), prev8(bwd), next8(bwd),
            const((4, D_RNN)), const((1, D_RNN)),
            const((2, RNN_BLOCKS, RNN_BW, RNN_BW)), const((2, 1, D_RNN)),
            const((2, RNN_BLOCKS, RNN_BW, RNN_BW)), const((2, 1, D_RNN)),
            const((2, 1, D_RNN)),
            pl.BlockSpec((1, 2, D_RNN), lambda b, i: (b, 0, 0)),
        ],
        out_specs=[main(fwd), main(bwd)],
        out_shape=[jax.ShapeDtypeStruct((bsz, seq, D_RNN), F32)] * 2,
        scratch_shapes=[
            pltpu.VMEM((per, SUBLANES, D_RNN), F32),
            pltpu.VMEM((per, SUBLANES, D_RNN), F32),
            pltpu.VMEM((SUBLANES, D_RNN), F32),
            pltpu.VMEM((SUBLANES, D_RNN), F32),
        ],
        compiler_params=pltpu.CompilerParams(
            dimension_semantics=("arbitrary", "arbitrary"), vmem_limit_bytes=VMEM_LIMIT),
        name="rglru",
    )(x_rnn, x_rnn, x_rnn, x_rnn, x_rnn, x_rnn, lw["conv_w"], lw["conv_b"], lw["lru_wa"], lw["lru_ba"],
      lw["lru_wi"], lw["lru_bi"], lw["lru_lam"], h0)


MLA_AHEAD = 2


def _mla_kernel(*refs, has_ctx, kc, par):
    if has_ctx:
        qt_ref, k_ref, vt_ref, kc_ref, vtc_ref, o_ref = refs
    else:
        qt_ref, k_ref, vt_ref, o_ref = refs
    seq = k_ref.shape[2]
    chunks = [(kc_ref, vtc_ref, 0, kc_ref.shape[2])] if has_ctx else []
    chunks += [(k_ref, vt_ref, c * kc, kc) for c in range(seq // kc)]

    def heads(j, carry):
        hs = [j * par + t for t in range(par)]
        qs = [qt_ref[0, h] for h in hs]
        ms = [None] * par
        accs = [None] * par
        tasks = [(t, c) for c in chunks for t in range(par)]

        def scores(task):
            t, (kr, _, off, n) = task
            return _dot(kr[0, hs[t], off:off + n, :], qs[t])

        pending = [scores(task) for task in tasks[:MLA_AHEAD]]
        for i, (t, (_, vr, off, n)) in enumerate(tasks):
            s = pending.pop(0)
            if i + MLA_AHEAD < len(tasks):
                pending.append(scores(tasks[i + MLA_AHEAD]))
            cm = jnp.max(s, axis=0, keepdims=True)
            m_new = cm if ms[t] is None else jnp.maximum(ms[t], cm)
            p = jnp.exp2(s - m_new).astype(BF16)
            pv = _dot(vr[0, hs[t], :, off:off + n], p)
            accs[t] = pv if ms[t] is None else jnp.exp2(ms[t] - m_new) * accs[t] + pv
            ms[t] = m_new
        for t, h in enumerate(hs):
            o_ref[0, h] = accs[t][:MLA_V] / accs[t][MLA_V:MLA_V + 1]
        return carry

    lax.fori_loop(0, MLA_HEADS // par, heads, 0)


def _mla(q_t, k, v_t, ctx, *, tq, kc):
    bsz, heads, _, seq = q_t.shape
    in_specs = [
        pl.BlockSpec((1, heads, MLA_HEAD_PAD, tq), lambda b, i: (b, 0, 0, i)),
        pl.BlockSpec((1, heads, seq, MLA_HEAD_PAD), lambda b, i: (b, 0, 0, 0)),
        pl.BlockSpec((1, heads, V_ROWS, seq), lambda b, i: (b, 0, 0, 0)),
    ]
    args = [q_t, k, v_t]
    if ctx is not None:
        past = ctx[0].shape[2]
        in_specs += [
            pl.BlockSpec((1, heads, past, MLA_HEAD_PAD), lambda b, i: (b, 0, 0, 0)),
            pl.BlockSpec((1, heads, V_ROWS, past), lambda b, i: (b, 0, 0, 0)),
        ]
        args += list(ctx)
    out = pl.pallas_call(
        functools.partial(_mla_kernel, has_ctx=ctx is not None, kc=kc, par=1),
        grid=(bsz, seq // tq),
        in_specs=in_specs,
        out_specs=pl.BlockSpec((1, heads, MLA_V, tq), lambda b, i: (b, 0, 0, i)),
        out_shape=jax.ShapeDtypeStruct((bsz, heads, MLA_V, seq), F32),
        compiler_params=pltpu.CompilerParams(vmem_limit_bytes=VMEM_LIMIT),
        name="mla_latent" if ctx is not None else "mla_context",
    )(*args)
    return out.reshape(bsz, heads * MLA_V, seq)


def _swa_kernel(*refs, windowed, tq, seq):
    if windowed:
        sink_ref, qt_ref, k_ref, vt_ref, kc_ref, vtc_ref, o_ref = refs
    else:
        sink_ref, qt_ref, kc_ref, vtc_ref, o_ref = refs
    i = pl.program_id(1)
    start = i * tq
    nq = SWA_GROUPS * tq
    lane = lax.broadcasted_iota(jnp.int32, (1, nq), 1)
    if windowed:
        win = tq + 2 * WINDOW
        ws = pl.multiple_of(jnp.clip(start - WINDOW, 0, seq - win), LANES)
        kpos = ws + lax.broadcasted_iota(jnp.int32, (win, 1), 0)
        qpos = start + lax.broadcasted_iota(jnp.int32, (1, tq), 1)
        valid = jnp.abs(kpos - qpos) <= WINDOW
        kw = k_ref[0, pl.ds(ws, win), :]
    zeros = jnp.zeros((SWA_HD, nq), BF16)
    scores = []
    for g in range(SWA_KV_HEADS):
        qg = jnp.concatenate(
            [qt_ref[0, (g * SWA_GROUPS + j) * SWA_HD:(g * SWA_GROUPS + j + 1) * SWA_HD, :] for j in range(SWA_GROUPS)],
            axis=1)
        qp = jnp.concatenate([qg, zeros] if g == 0 else [zeros, qg], axis=0)
        sc = _dot(kc_ref[0], qp)
        scores.append((sc, _dot(kw, qp) if windowed else None))
    for g in range(SWA_KV_HEADS):
        sc, s = scores[g]
        sk = jnp.zeros((1, nq), F32)
        for j in range(SWA_GROUPS):
            sk = jnp.where(lane // tq == j, sink_ref[g * SWA_GROUPS + j] * LOG2E, sk)
        m = jnp.maximum(jnp.max(sc, axis=0, keepdims=True), sk)
        if windowed:
            s = jnp.concatenate(
                [jnp.where(valid, s[:, j * tq:(j + 1) * tq], NEG_INF) for j in range(SWA_GROUPS)], axis=1)
            m = jnp.maximum(m, jnp.max(s, axis=0, keepdims=True))
        acc = _dot(vtc_ref[0, g], jnp.exp2(sc - m).astype(BF16))
        if windowed:
            acc = acc + _dot(vt_ref[0, g, :, pl.ds(ws, win)], jnp.exp2(s - m).astype(BF16))
        o = acc[:SWA_HD] / (acc[SWA_HD:SWA_HD + 1] + jnp.exp2(sk - m))
        for j in range(SWA_GROUPS):
            hd = g * SWA_GROUPS + j
            o_ref[0, hd * SWA_HD:(hd + 1) * SWA_HD, :] = o[:, j * tq:(j + 1) * tq]


def _swa(sink, qs_t, ks, vs_t, ctx, *, tq):
    bsz, _, seq = qs_t.shape
    windowed = ctx is not None
    whole_k = lambda n: pl.BlockSpec((1, n, LANES), lambda b, i: (b, 0, 0))
    whole_vt = lambda n: pl.BlockSpec((1, SWA_KV_HEADS, V_ROWS, n), lambda b, i: (b, 0, 0, 0))
    in_specs = [
        pl.BlockSpec(memory_space=pltpu.SMEM),
        pl.BlockSpec((1, SWA_WIDTH, tq), lambda b, i: (b, 0, i)),
        whole_k(seq), whole_vt(seq),
    ]
    args = [sink, qs_t, ks, vs_t]
    if windowed:
        past = ctx[0].shape[1]
        in_specs += [whole_k(past), whole_vt(past)]
        args += list(ctx)
    return pl.pallas_call(
        functools.partial(_swa_kernel, windowed=windowed, tq=tq, seq=seq),
        grid=(bsz, seq // tq),
        in_specs=in_specs,
        out_specs=pl.BlockSpec((1, SWA_WIDTH, tq), lambda b, i: (b, 0, i)),
        out_shape=jax.ShapeDtypeStruct((bsz, SWA_WIDTH, seq), F32),
        compiler_params=pltpu.CompilerParams(vmem_limit_bytes=VMEM_LIMIT),
        name="swa_latent" if windowed else "swa_context",
    )(*args)


def _silu(x):
    return x * jax.nn.sigmoid(x)


def _merge_kernel(*refs, final):
    if final:
        (x_ref, mod_ref, gn_ref, hf_ref, hb_ref, ym_ref, ys_ref, wg_ref, wr_ref, wm_ref, ws_ref, wo_ref, fn_ref,
         o_ref) = refs
    else:
        (x_ref, mod_ref, gn_ref, hf_ref, hb_ref, ym_ref, ys_ref, wg_ref, wr_ref, wm_ref, ws_ref, wo_ref,
         o_ref) = refs
    x = x_ref[0]
    mod = mod_ref[0]
    h = _norm_mod(x, gn_ref[...], mod).astype(BF16)
    g = _dot(h, wg_ref[...])
    o = 0
    g_rnn = g[:, o:o + D_RNN]
    o += D_RNN
    g_mla = g[:, o:o + MLA_WIDTH]
    o += MLA_WIDTH
    g_swa = g[:, o:o + SWA_WIDTH]
    o += SWA_WIDTH
    y_rnn = hf_ref[0] + hb_ref[0]
    y_mla = ym_ref[0].T
    y_swa = ys_ref[0].T
    b_rnn = _dot((y_rnn * _silu(g_rnn)).astype(BF16), wr_ref[...])
    b_mla = _dot((y_mla * _silu(g_mla)).astype(BF16), wm_ref[...])
    b_swa = _dot((y_swa * _silu(g_swa)).astype(BF16), ws_ref[...])
    u = (jax.nn.sigmoid(g[:, o:o + D_MODEL]) * b_rnn
         + jax.nn.sigmoid(g[:, o + D_MODEL:o + 2 * D_MODEL]) * b_mla
         + jax.nn.sigmoid(g[:, o + 2 * D_MODEL:o + 3 * D_MODEL]) * b_swa)
    out = _dot(u.astype(BF16), wo_ref[...])
    xn = x + mod[2:3] * out
    if final:
        xn = _rms(xn, fn_ref[...])
    o_ref[0] = xn


def _merge(x, mod, lw, h_f, h_b, ym_t, ys_t, final_norm, *, tm):
    bsz, seq, _ = x.shape
    per_batch_mod = mod.shape[0] > 1
    final = final_norm is not None
    tok3 = lambda w: pl.BlockSpec((1, tm, w), lambda b, i: (b, i, 0))
    feat3 = lambda f: pl.BlockSpec((1, f, tm), lambda b, i: (b, 0, i))
    in_specs = [
        tok3(D_MODEL),
        pl.BlockSpec((1, 3, D_MODEL), (lambda b, i: (b, 0, 0)) if per_batch_mod else (lambda b, i: (0, 0, 0))),
        _const_spec((1, D_MODEL)),
        tok3(D_RNN), tok3(D_RNN), feat3(MLA_WIDTH), feat3(SWA_WIDTH),
        _const_spec((D_MODEL, N_GATES)),
        _const_spec((D_RNN, D_MODEL)),
        _const_spec((MLA_WIDTH, D_MODEL)),
        _const_spec((SWA_WIDTH, D_MODEL)),
        _const_spec((D_MODEL, D_MODEL)),
    ]
    args = [x, mod, lw["g_norm"], h_f, h_b, ym_t, ys_t, lw["w_gates"], lw["w_br_rnn"], lw["w_br_mla"], lw["w_br_swa"],
            lw["w_out"]]
    if final:
        in_specs.append(_const_spec((1, D_MODEL)))
        args.append(final_norm)
    return pl.pallas_call(
        functools.partial(_merge_kernel, final=final),
        grid=(bsz, seq // tm),
        in_specs=in_specs,
        out_specs=tok3(D_MODEL),
        out_shape=jax.ShapeDtypeStruct((bsz, seq, D_MODEL), F32),
        compiler_params=pltpu.CompilerParams(vmem_limit_bytes=VMEM_LIMIT),
        name="merge_final" if final else "merge",
    )(*args)


def _split_points():
    sizes = (D_RNN, D_RNN, Q_LORA, KV_LORA, MLA_ROPE, MLA_WIDTH, SWA_WIDTH, SWA_KV_HEADS * SWA_HD,
             SWA_KV_HEADS * SWA_HD, SWA_WIDTH, 3 * D_MODEL)
    pts, acc = [0], 0
    for s in sizes:
        acc += s
        pts.append(acc)
    return pts


def _layer_weights(l, g_norm, w_in, conv_w, conv_b, lru_wa, lru_ba, lru_wi, lru_bi, lru_lam, mla_q_norm, mla_w_uq,
                   mla_kv_norm, mla_w_ukv, w_br_rnn, w_br_mla, w_br_swa, w_out):
    p = _split_points()
    w = w_in[l]
    seg = lambda k: w[:, p[k]:p[k + 1]]
    (w_xr, w_grnn, w_cq, w_ckv, w_kr, w_gmla, w_qs, w_ks, w_vs, w_gswa, w_ml) = [seg(k) for k in range(11)]
    w_kr_blk = jnp.pad(w_kr, ((0, 0), (MLA_NOPE, MLA_HEAD_PAD - MLA_NOPE - MLA_ROPE)))
    w_row_lat = jnp.concatenate([w_xr, w_cq, w_ckv, w_kr_blk, w_ks], axis=1).astype(BF16)
    w_row_ctx = jnp.concatenate([w_xr, w_cq, w_ckv, w_kr_blk, w_ks, w_vs], axis=1).astype(BF16)
    w_t = jnp.concatenate([w_qs, w_vs], axis=1).T.astype(BF16)
    w_gates = jnp.concatenate([w_grnn, w_gmla, w_gswa, w_ml], axis=1).astype(BF16)
    uq = mla_w_uq[l].reshape(Q_LORA, MLA_HEADS, MLA_NOPE + MLA_ROPE)
    uq = jnp.pad(uq, ((0, 0), (0, 0), (0, MLA_HEAD_PAD - MLA_NOPE - MLA_ROPE)))
    w_uq_t = uq.reshape(Q_LORA, MLA_HEADS * MLA_HEAD_PAD).T.astype(BF16)
    ukv = mla_w_ukv[l].reshape(KV_LORA, MLA_HEADS, MLA_NOPE + MLA_V)
    uk = jnp.pad(ukv[:, :, :MLA_NOPE], ((0, 0), (0, 0), (0, MLA_HEAD_PAD - MLA_NOPE)))
    w_uk = uk.reshape(KV_LORA, MLA_HEADS * MLA_HEAD_PAD).astype(BF16)
    w_uv_t = ukv[:, :, MLA_NOPE:].reshape(KV_LORA, MLA_WIDTH).T.astype(BF16)
    return dict(
        g_norm=g_norm[l].reshape(1, D_MODEL), w_row_lat=w_row_lat, w_row_ctx=w_row_ctx, w_t=w_t, w_gates=w_gates,
        g_q=mla_q_norm[l].reshape(1, Q_LORA), g_kv=mla_kv_norm[l].reshape(1, KV_LORA),
        w_uq_t=w_uq_t, w_uk=w_uk, w_uv_t=w_uv_t,
        conv_w=conv_w[l], conv_b=conv_b[l].reshape(1, D_RNN),
        lru_wa=lru_wa[l].astype(BF16), lru_ba=lru_ba[l].reshape(2, 1, D_RNN),
        lru_wi=lru_wi[l].astype(BF16), lru_bi=lru_bi[l].reshape(2, 1, D_RNN),
        lru_lam=lru_lam[l].reshape(2, 1, D_RNN),
        w_br_rnn=w_br_rnn[l].astype(BF16), w_br_mla=w_br_mla[l].astype(BF16), w_br_swa=w_br_swa[l].astype(BF16),
        w_out=w_out[l].astype(BF16),
    )


def _rope_tables(n_tok, rot_dim):
    rows = n_tok // GRID_W
    row = jnp.repeat(jnp.arange(rows, dtype=F32), GRID_W)
    col = jnp.tile(jnp.arange(GRID_W, dtype=F32), rows)
    half = rot_dim // 2
    freqs = ROPE_BASE ** (-jnp.arange(0, half, 2, dtype=F32) / half)
    ang_r = row[:, None] * freqs[None, :]
    ang_c = col[:, None] * freqs[None, :]
    ang = jnp.concatenate([ang_r, ang_r, ang_c, ang_c], axis=-1)
    q = rot_dim // 4
    sign = jnp.concatenate([-jnp.ones((q,), F32), jnp.ones((q,), F32)] * 2)
    return jnp.cos(ang), jnp.sin(ang) * sign


def _rope_inputs(n_tok):
    cos_m, sin_m = _rope_tables(n_tok, MLA_ROPE)
    cos_s, sin_s = _rope_tables(n_tok, SWA_HD)
    pad_m = ((0, 0), (MLA_NOPE, MLA_HEAD_PAD - MLA_NOPE - MLA_ROPE))
    return (cos_m.T, sin_m.T, jnp.pad(cos_m, pad_m, constant_values=1.0), jnp.pad(sin_m, pad_m),
            cos_s.T, sin_s.T, jnp.tile(cos_s, (1, SWA_KV_HEADS)), jnp.tile(sin_s, (1, SWA_KV_HEADS)))


def _tile(n, pref):
    return pref if n % pref == 0 else n


def _context_layer(x, mod, lw, sink, final_norm):
    bsz, seq, _ = x.shape
    lw = dict(lw, w_row=lw["w_row_ctx"])
    (x_rnn, q_t, k, v_t, qs_t, ks, vs_t, ckv, kr_blk, ks_raw, vs_raw) = _proj(
        x, mod, lw, None, latent=False, tm=_tile(seq, 256))
    h_f, h_b = _rglru(x_rnn, jnp.zeros((bsz, 2, D_RNN), F32), lw, tl=_tile(seq, 256))
    ym_t = _mla(q_t, k, v_t, None, tq=_tile(seq, 256), kc=_tile(seq, 512))
    ys_t = _swa(sink, qs_t, ks, vs_t, None, tq=_tile(seq, 256))
    x = _merge(x, mod, lw, h_f, h_b, ym_t, ys_t, final_norm, tm=_tile(seq, 256))
    h_state = jnp.stack([h_f[:, -1], h_b[:, 0]], axis=1)
    k_rope = kr_blk[:, :, MLA_NOPE:MLA_NOPE + MLA_ROPE]
    return (x, ckv, k_rope, ks_raw.reshape(bsz, seq, SWA_KV_HEADS, SWA_HD),
            vs_raw.reshape(bsz, seq, SWA_KV_HEADS, SWA_HD), h_state)


def _latent_layer(x, mod, lw, sink, rope, ckv_ctx, krope_ctx, k_ctx, v_ctx, h_ctx, final_norm):
    bsz, seq, _ = x.shape
    past = ckv_ctx.shape[1]
    lw = dict(lw, w_row=lw["w_row_lat"])
    x_rnn, q_t, k, v_t, qs_t, ks, vs_t = _proj(x, mod, lw, rope, latent=True, tm=_tile(seq, 512))
    kr_blk = jnp.pad(krope_ctx, ((0, 0), (0, 0), (MLA_NOPE, MLA_HEAD_PAD - MLA_NOPE - MLA_ROPE)))
    kc, vc_t = _cache_proj(ckv_ctx, kr_blk, lw)
    h_f, h_b = _rglru(x_rnn, h_ctx, lw, tl=_tile(seq, 256))
    ym_t = _mla(q_t, k, v_t, (kc, vc_t), tq=_tile(seq, 512), kc=_tile(seq, 512))
    ks_c = k_ctx.reshape(bsz, past, SWA_KV_HEADS * SWA_HD).astype(BF16)
    vs_c_t = jnp.concatenate([jnp.transpose(v_ctx, (0, 2, 3, 1)).astype(BF16),
                              jnp.ones((bsz, SWA_KV_HEADS, ONES_ROWS, past), BF16)], axis=2)
    ys_t = _swa(sink, qs_t, ks, vs_t, (ks_c, vs_c_t), tq=_tile(seq, 256))
    return _merge(x, mod, lw, h_f, h_b, ym_t, ys_t, final_norm, tm=_tile(seq, 256))


def kernel(x_prompt, x_sample, cache_mla_ckv, cache_mla_krope, cache_swa_k, cache_swa_v, state_rglru, c, c_ctx, w_mod, b_mod, g_norm, w_in, conv_w, conv_b, lru_wa, lru_ba, lru_wi, lru_bi, lru_lam, mla_q_norm, mla_w_uq, mla_kv_norm, mla_w_ukv, swa_sink, w_br_rnn, w_br_mla, w_br_swa, w_out, final_norm):
    depth = w_in.shape[0]
    dec_b = x_sample.shape[0]
    n_cond = -(-(dec_b + 1) // SUBLANES) * SUBLANES
    conds = jnp.concatenate([c, c_ctx[None, :], jnp.zeros((n_cond - dec_b - 1, D_MODEL), F32)], axis=0)
    mod = _modulation(conds, w_mod, b_mod).reshape(depth, n_cond, 3, D_MODEL)
    fn = final_norm.reshape(1, D_MODEL)
    layers = [_layer_weights(l, g_norm, w_in, conv_w, conv_b, lru_wa, lru_ba, lru_wi, lru_bi, lru_lam, mla_q_norm,
                             mla_w_uq, mla_kv_norm, mla_w_ukv, w_br_rnn, w_br_mla, w_br_swa, w_out)
              for l in range(depth)]

    xp = x_prompt
    ckvs, krs, sks, svs, hs = [], [], [], [], []
    for l in range(depth):
        xp, ckv, kr, sk, sv, hst = _context_layer(xp, mod[l, dec_b:dec_b + 1], layers[l], swa_sink[l],
                                                  fn if l == depth - 1 else None)
        ckvs.append(ckv)
        krs.append(kr)
        sks.append(sk)
        svs.append(sv)
        hs.append(hst)

    rope = _rope_inputs(x_sample.shape[1])
    xs = x_sample
    for l in range(depth):
        xs = _latent_layer(xs, mod[l, :dec_b], layers[l], swa_sink[l], rope, cache_mla_ckv[:, l],
                           cache_mla_krope[:, l], cache_swa_k[:, l], cache_swa_v[:, l], state_rglru[:, l],
                           fn if l == depth - 1 else None)
    return (xp, xs, jnp.stack(ckvs, axis=1), jnp.stack(krs, axis=1), jnp.stack(sks, axis=1),
            jnp.stack(svs, axis=1), jnp.stack(hs, axis=1))
```

```python
import functools

import jax
import jax.numpy as jnp
from jax import lax
from jax.experimental import pallas as pl
from jax.experimental.pallas import tpu as pltpu

F32 = jnp.float32
BF16 = jnp.bfloat16

D_MODEL = 1024
GRID_W = 64
EPS = 1e-6
ROPE_BASE = 10000.0
NEG_INF = -1e30
D_RNN = 1024
RNN_BLOCKS = 8
RNN_BW = D_RNN // RNN_BLOCKS
LRU_C = 8.0
MLA_HEADS = 8
MLA_NOPE = 64
MLA_ROPE = 32
MLA_V = 64
MLA_HEAD_PAD = 128
Q_LORA = 384
KV_LORA = 256
MLA_WIDTH = MLA_HEADS * MLA_V
SWA_HEADS = 8
SWA_KV_HEADS = 2
SWA_GROUPS = SWA_HEADS // SWA_KV_HEADS
SWA_HD = 64
WINDOW = 128
SWA_WIDTH = SWA_HEADS * SWA_HD
N_GATES = D_RNN + MLA_WIDTH + SWA_WIDTH + 3 * D_MODEL
LOG2E = 1.4426950408889634
MLA_QK_SCALE = (MLA_NOPE + MLA_ROPE) ** -0.5 * LOG2E
SWA_QK_SCALE = SWA_HD ** -0.5 * LOG2E
ONES_ROWS = 16
V_ROWS = MLA_V + ONES_ROWS

SUBLANES = 8
LANES = 128
VMEM_LIMIT = 56 * 1024 * 1024


def _dot(a, b):
    return jnp.dot(a, b, preferred_element_type=F32)


def _dot_nt(a, b):
    return lax.dot_general(a, b, (((1,), (1,)), ((), ())), preferred_element_type=F32)


def _rms(x, g):
    ms = jnp.mean(x * x, axis=-1, keepdims=True)
    return x * lax.rsqrt(ms + EPS) * g


def _norm_mod(x, g, mod):
    return _rms(x, g) * (1.0 + mod[1:2]) + mod[0:1]


def _swap_row_chunks(x, q):
    parts = []
    for c in range(x.shape[0] // (2 * q)):
        parts.append(x[(2 * c + 1) * q:(2 * c + 2) * q])
        parts.append(x[2 * c * q:(2 * c + 1) * q])
    return jnp.concatenate(parts, axis=0)


def _swap_lane_chunks(x, q):
    w = x.shape[-1]
    lane = lax.broadcasted_iota(jnp.int32, x.shape, x.ndim - 1)
    first = (lane % (2 * q)) < q
    return jnp.where(first, pltpu.roll(x, w - q, x.ndim - 1), pltpu.roll(x, q, x.ndim - 1))


def _mod_kernel(c_ref, w_ref, b_ref, o_ref):
    c = c_ref[...]
    a = (c * jax.nn.sigmoid(c)).astype(BF16)
    o_ref[0] = _dot(a, w_ref[0].astype(BF16)) + b_ref[0]


def _modulation(conds, w_mod, b_mod):
    depth = w_mod.shape[0]
    n = conds.shape[0]
    tn = 1024
    return pl.pallas_call(
        _mod_kernel,
        grid=(depth, 3 * D_MODEL // tn),
        in_specs=[
            pl.BlockSpec((n, D_MODEL), lambda l, j: (0, 0)),
            pl.BlockSpec((1, D_MODEL, tn), lambda l, j: (l, 0, j)),
            pl.BlockSpec((1, 1, tn), lambda l, j: (l, 0, j)),
        ],
        out_specs=pl.BlockSpec((1, n, tn), lambda l, j: (l, 0, j)),
        out_shape=jax.ShapeDtypeStruct((depth, n, 3 * D_MODEL), F32),
        compiler_params=pltpu.CompilerParams(vmem_limit_bytes=VMEM_LIMIT),
        name="modulation",
    )(conds, w_mod, b_mod.reshape(depth, 1, 3 * D_MODEL))


N_ROW_LAT = D_RNN + Q_LORA + KV_LORA + LANES + LANES
N_ROW_CTX = N_ROW_LAT + LANES
N_T = SWA_WIDTH + SWA_KV_HEADS * SWA_HD


def _centred_conv(x, prev_row, next_rows, cw_ref, cb_ref, out_ref):
    tl = x.shape[0]
    w0, w1, w2, w3, b = cw_ref[0:1], cw_ref[1:2], cw_ref[2:3], cw_ref[3:4], cb_ref[...]
    out_ref[0] = w0 * pltpu.roll(x, 1, 0) + w1 * x + w2 * pltpu.roll(x, tl - 1, 0) + w3 * pltpu.roll(x, tl - 2, 0) + b
    row = lambda t: x[t:t + 1]
    out_ref[0, 0:1, :] = w0 * prev_row + w1 * row(0) + w2 * row(1) + w3 * row(2) + b
    out_ref[0, tl - 2:tl - 1, :] = w0 * row(tl - 3) + w1 * row(tl - 2) + w2 * row(tl - 1) + w3 * next_rows[0:1] + b
    out_ref[0, tl - 1:tl, :] = w0 * row(tl - 2) + w1 * row(tl - 1) + w2 * next_rows[0:1] + w3 * next_rows[1:2] + b


def _proj_kernel(*refs, latent):
    if latent:
        (x_ref, xp_ref, xn_ref, mod_ref, gn_ref, wrow_ref, wt_ref, gq_ref, gkv_ref, wuqt_ref, wuk_ref, wuvt_ref,
         cw_ref, cb_ref, cq_ref, sq_ref, ck_ref, sk_ref, cs_ref, ss_ref, cks_ref, sks_ref,
         xr_ref, qt_ref, k_ref, vt_ref, qst_ref, ks_ref, vst_ref) = refs
    else:
        (x_ref, xp_ref, xn_ref, mod_ref, gn_ref, wrow_ref, wt_ref, gq_ref, gkv_ref, wuqt_ref, wuk_ref, wuvt_ref,
         cw_ref, cb_ref,
         xr_ref, qt_ref, k_ref, vt_ref, qst_ref, ks_ref, vst_ref,
         ckv_ref, kr_ref, ksraw_ref, vsraw_ref) = refs
    i = pl.program_id(1)
    last = pl.num_programs(1) - 1

    h = _norm_mod(x_ref[0], gn_ref[...], mod_ref[0]).astype(BF16)
    halo = jnp.concatenate([xp_ref[0], xn_ref[0]], axis=0)
    hh = _norm_mod(halo, gn_ref[...], mod_ref[0]).astype(BF16)
    xr_halo = _dot(hh, wrow_ref[:, 0:D_RNN])
    prev_row = jnp.where(i > 0, xr_halo[SUBLANES - 1:SUBLANES], 0.0)
    next_rows = jnp.where(i < last, xr_halo[SUBLANES:SUBLANES + 2], 0.0)
    x_rnn = _dot(h, wrow_ref[:, 0:D_RNN])
    r = _dot(h, wrow_ref[:, D_RNN:])
    _centred_conv(x_rnn, prev_row, next_rows, cw_ref, cb_ref, xr_ref)
    o = 0
    c_q = r[:, o:o + Q_LORA]
    o += Q_LORA
    c_kv = r[:, o:o + KV_LORA]
    o += KV_LORA
    kr = r[:, o:o + LANES]
    o += LANES
    ks = r[:, o:o + LANES]
    o += LANES
    t = _dot_nt(wt_ref[...], h)
    qs_t = t[:SWA_WIDTH] * SWA_QK_SCALE
    vs_t = t[SWA_WIDTH:]
    ones = jnp.ones((ONES_ROWS, x_ref.shape[1]), BF16)

    cqn = _rms(c_q, gq_ref[...]).astype(BF16)
    q_t = _dot_nt(wuqt_ref[...], cqn) * MLA_QK_SCALE
    for hd in range(MLA_HEADS):
        base = hd * MLA_HEAD_PAD
        blk = q_t[base:base + MLA_HEAD_PAD]
        if latent:
            rp = blk[MLA_NOPE:MLA_NOPE + MLA_ROPE]
            rp = rp * cq_ref[...] + _swap_row_chunks(rp, MLA_ROPE // 4) * sq_ref[...]
            blk = jnp.concatenate([blk[:MLA_NOPE], rp, blk[MLA_NOPE + MLA_ROPE:]], axis=0)
        qt_ref[0, hd] = blk.astype(BF16)

    ckv = _rms(c_kv, gkv_ref[...])
    ckv16 = ckv.astype(BF16)
    if latent:
        kr = kr * ck_ref[...] + _swap_lane_chunks(kr, MLA_ROPE // 4) * sk_ref[...]
    kn = _dot(ckv16, wuk_ref[...])
    for hd in range(MLA_HEADS):
        k_ref[0, hd] = (kn[:, hd * MLA_HEAD_PAD:(hd + 1) * MLA_HEAD_PAD] + kr).astype(BF16)
    v_t = _dot_nt(wuvt_ref[...], ckv16)
    for hd in range(MLA_HEADS):
        vt_ref[0, hd] = jnp.concatenate([v_t[hd * MLA_V:(hd + 1) * MLA_V].astype(BF16), ones], axis=0)

    if latent:
        parts = []
        for hd in range(SWA_HEADS):
            qh = qs_t[hd * SWA_HD:(hd + 1) * SWA_HD]
            parts.append(qh * cs_ref[...] + _swap_row_chunks(qh, SWA_HD // 4) * ss_ref[...])
        qs_t = jnp.concatenate(parts, axis=0)
        ks_r = ks * cks_ref[...] + _swap_lane_chunks(ks, SWA_HD // 4) * sks_ref[...]
    else:
        ks_r = ks
    qst_ref[0] = qs_t.astype(BF16)
    ks_ref[0] = ks_r.astype(BF16)
    for g in range(SWA_KV_HEADS):
        vst_ref[0, g] = jnp.concatenate([vs_t[g * SWA_HD:(g + 1) * SWA_HD].astype(BF16), ones], axis=0)

    if not latent:
        ckv_ref[0] = ckv
        kr_ref[0] = kr
        ksraw_ref[0] = ks
        vsraw_ref[0] = r[:, o:o + LANES]


def _layer_spec(shape, l):
    nd = len(shape)
    return pl.BlockSpec((None,) + tuple(shape), lambda *_: (l,) + (0,) * nd, pipeline_mode=pl.Buffered(1))


def _proj(x, mod, lw, rope, *, l, latent, tm):
    bsz, seq, _ = x.shape
    per_batch_mod = mod.shape[0] > 1
    n_row = N_ROW_LAT if latent else N_ROW_CTX
    _const_spec = functools.partial(_layer_spec, l=l)
    tok3 = lambda w: pl.BlockSpec((1, tm, w), lambda b, i: (b, i, 0))
    feat3 = lambda f: pl.BlockSpec((1, f, tm), lambda b, i: (b, 0, i))
    per = tm // SUBLANES
    rows8 = seq // SUBLANES
    in_specs = [
        tok3(D_MODEL),
        pl.BlockSpec((1, SUBLANES, D_MODEL), lambda b, i: (b, jnp.maximum(i * per - 1, 0), 0)),
        pl.BlockSpec((1, SUBLANES, D_MODEL), lambda b, i: (b, jnp.minimum((i + 1) * per, rows8 - 1), 0)),
        pl.BlockSpec((1, 3, D_MODEL), (lambda b, i: (b, 0, 0)) if per_batch_mod else (lambda b, i: (0, 0, 0))),
        _const_spec((1, D_MODEL)),
        _const_spec((D_MODEL, n_row)),
        _const_spec((N_T, D_MODEL)),
        _const_spec((1, Q_LORA)),
        _const_spec((1, KV_LORA)),
        _const_spec((MLA_HEADS * MLA_HEAD_PAD, Q_LORA)),
        _const_spec((KV_LORA, MLA_HEADS * MLA_HEAD_PAD)),
        _const_spec((MLA_WIDTH, KV_LORA)),
        _const_spec((4, D_RNN)),
        _const_spec((1, D_RNN)),
    ]
    args = [x, x, x, mod, lw["g_norm"], lw["w_row_lat" if latent else "w_row_ctx"], lw["w_t"], lw["g_q"], lw["g_kv"],
            lw["w_uq_t"], lw["w_uk"], lw["w_uv_t"], lw["conv_w"], lw["conv_b"]]
    if latent:
        in_specs += [
            pl.BlockSpec((MLA_ROPE, tm), lambda b, i: (0, i)),
            pl.BlockSpec((MLA_ROPE, tm), lambda b, i: (0, i)),
            pl.BlockSpec((tm, LANES), lambda b, i: (i, 0)),
            pl.BlockSpec((tm, LANES), lambda b, i: (i, 0)),
            pl.BlockSpec((SWA_HD, tm), lambda b, i: (0, i)),
            pl.BlockSpec((SWA_HD, tm), lambda b, i: (0, i)),
            pl.BlockSpec((tm, LANES), lambda b, i: (i, 0)),
            pl.BlockSpec((tm, LANES), lambda b, i: (i, 0)),
        ]
        args += list(rope)
    out_specs = [
        tok3(D_RNN),
        pl.BlockSpec((1, MLA_HEADS, MLA_HEAD_PAD, tm), lambda b, i: (b, 0, 0, i)),
        pl.BlockSpec((1, MLA_HEADS, tm, MLA_HEAD_PAD), lambda b, i: (b, 0, i, 0)),
        pl.BlockSpec((1, MLA_HEADS, V_ROWS, tm), lambda b, i: (b, 0, 0, i)),
        feat3(SWA_WIDTH),
        tok3(LANES),
        pl.BlockSpec((1, SWA_KV_HEADS, V_ROWS, tm), lambda b, i: (b, 0, 0, i)),
    ]
    out_shape = [
        jax.ShapeDtypeStruct((bsz, seq, D_RNN), F32),
        jax.ShapeDtypeStruct((bsz, MLA_HEADS, MLA_HEAD_PAD, seq), BF16),
        jax.ShapeDtypeStruct((bsz, MLA_HEADS, seq, MLA_HEAD_PAD), BF16),
        jax.ShapeDtypeStruct((bsz, MLA_HEADS, V_ROWS, seq), BF16),
        jax.ShapeDtypeStruct((bsz, SWA_WIDTH, seq), BF16),
        jax.ShapeDtypeStruct((bsz, seq, LANES), BF16),
        jax.ShapeDtypeStruct((bsz, SWA_KV_HEADS, V_ROWS, seq), BF16),
    ]
    if not latent:
        out_specs += [tok3(KV_LORA), tok3(LANES), tok3(LANES), tok3(LANES)]
        out_shape += [
            jax.ShapeDtypeStruct((bsz, seq, KV_LORA), F32),
            jax.ShapeDtypeStruct((bsz, seq, LANES), F32),
            jax.ShapeDtypeStruct((bsz, seq, LANES), F32),
            jax.ShapeDtypeStruct((bsz, seq, LANES), F32),
        ]
    return pl.pallas_call(
        functools.partial(_proj_kernel, latent=latent),
        grid=(bsz, seq // tm),
        in_specs=in_specs,
        out_specs=out_specs,
        out_shape=out_shape,
        compiler_params=pltpu.CompilerParams(vmem_limit_bytes=VMEM_LIMIT),
        name="proj_latent" if latent else "proj_context",
    )(*args)


def _cache_kernel(ckv_ref, kr_ref, wuk_ref, wuvt_ref, k_ref, vt_ref):
    ckv16 = ckv_ref[0].astype(BF16)
    kn = _dot(ckv16, wuk_ref[...])
    kr = kr_ref[0]
    for hd in range(MLA_HEADS):
        k_ref[0, hd] = (kn[:, hd * MLA_HEAD_PAD:(hd + 1) * MLA_HEAD_PAD] + kr).astype(BF16)
    v_t = _dot_nt(wuvt_ref[...], ckv16)
    ones = jnp.ones((ONES_ROWS, ckv_ref.shape[1]), BF16)
    for hd in range(MLA_HEADS):
        vt_ref[0, hd] = jnp.concatenate([v_t[hd * MLA_V:(hd + 1) * MLA_V].astype(BF16), ones], axis=0)


def _cache_proj(ckv, kr_blk, lw, *, l):
    bsz, _, past, _ = ckv.shape
    return pl.pallas_call(
        _cache_kernel,
        grid=(bsz,),
        in_specs=[
            pl.BlockSpec((1, None, past, KV_LORA), lambda b: (b, l, 0, 0)),
            pl.BlockSpec((1, None, past, LANES), lambda b: (b, l, 0, 0)),
            _layer_spec((KV_LORA, MLA_HEADS * MLA_HEAD_PAD), l),
            _layer_spec((MLA_WIDTH, KV_LORA), l),
        ],
        out_specs=[
            pl.BlockSpec((1, MLA_HEADS, past, MLA_HEAD_PAD), lambda b: (b, 0, 0, 0)),
            pl.BlockSpec((1, MLA_HEADS, V_ROWS, past), lambda b: (b, 0, 0, 0)),
        ],
        out_shape=[
            jax.ShapeDtypeStruct((bsz, MLA_HEADS, past, MLA_HEAD_PAD), BF16),
            jax.ShapeDtypeStruct((bsz, MLA_HEADS, V_ROWS, past), BF16),
        ],
        name="cache_proj",
    )(ckv, kr_blk, lw["w_uk"], lw["w_uv_t"])


def _log_sigmoid(x):
    return jnp.minimum(x, 0.0) - jnp.log1p(jnp.exp(-jnp.abs(x)))


def _rglru_direction(d, xc, wa_ref, ba_ref, wi_ref, bi_ref, lam_ref, a_scr, u_scr, carry, out_ref):
    tl = xc.shape[0]
    groups = tl // SUBLANES
    xcb = xc.astype(BF16)
    ra, ri = [], []
    for n in range(RNN_BLOCKS):
        blk = xcb[:, n * RNN_BW:(n + 1) * RNN_BW]
        ra.append(_dot(blk, wa_ref[d, n]))
        ri.append(_dot(blk, wi_ref[d, n]))
    th_r = jnp.tanh(jnp.concatenate(ra, axis=1) + ba_ref[d])
    th_i = jnp.tanh(jnp.concatenate(ri, axis=1) + bi_ref[d])
    half_c = (0.5 * LRU_C * LOG2E) * _log_sigmoid(lam_ref[d])
    a = jnp.exp2(half_c * th_r + half_c)
    u = jnp.sqrt(jnp.maximum(1.0 - a * a, 0.0)) * (th_i + 1.0) * (0.5 * xc)

    a3 = a.reshape(groups, SUBLANES, D_RNN)
    u3 = u.reshape(groups, SUBLANES, D_RNN)
    sub = lax.broadcasted_iota(jnp.int32, a3.shape, 1)
    for step in (1, 2, 4):
        if d == 0:
            ok = sub >= step
            shift = step
        else:
            ok = sub < SUBLANES - step
            shift = SUBLANES - step
        a_sh = jnp.where(ok, pltpu.roll(a3, shift, 1), 1.0)
        u_sh = jnp.where(ok, pltpu.roll(u3, shift, 1), 0.0)
        u3 = a3 * u_sh + u3
        a3 = a3 * a_sh
    a_scr[...] = a3
    u_scr[...] = u3

    def body(k, hp):
        g = k if d == 0 else groups - 1 - k
        hg = u_scr[g] + a_scr[g] * hp
        out_ref[0, pl.ds(pl.multiple_of(g * SUBLANES, SUBLANES), SUBLANES), :] = hg
        edge = hg[SUBLANES - 1:SUBLANES] if d == 0 else hg[0:1]
        return jnp.broadcast_to(edge, (SUBLANES, D_RNN))

    return lax.fori_loop(0, groups, body, carry, unroll=8)


def _rglru_kernel(xf_ref, xb_ref, wa_ref, ba_ref, wi_ref, bi_ref, lam_ref, h0_ref, hf_ref, hb_ref,
                  a_scr, u_scr, cf_scr, cbk_scr):
    @pl.when(pl.program_id(1) == 0)
    def _():
        cf_scr[...] = jnp.broadcast_to(h0_ref[0, 0:1], (SUBLANES, D_RNN))
        cbk_scr[...] = jnp.broadcast_to(h0_ref[0, 1:2], (SUBLANES, D_RNN))

    cf_scr[...] = _rglru_direction(0, xf_ref[0], wa_ref, ba_ref, wi_ref, bi_ref, lam_ref, a_scr, u_scr,
                                   cf_scr[...], hf_ref)
    cbk_scr[...] = _rglru_direction(1, xb_ref[0], wa_ref, ba_ref, wi_ref, bi_ref, lam_ref, a_scr, u_scr,
                                    cbk_scr[...], hb_ref)


def _rglru(xc, h0, lw, *, l, tl):
    bsz, seq, _ = xc.shape
    nc = seq // tl
    per = tl // SUBLANES
    main = lambda f: pl.BlockSpec((1, tl, D_RNN), lambda b, i: (b, f(i), 0))
    fwd = lambda i: i
    bwd = lambda i: nc - 1 - i
    const = functools.partial(_layer_spec, l=l)
    if h0.ndim == 4:
        h0_spec = pl.BlockSpec((1, None, 2, D_RNN), lambda b, i: (b, l, 0, 0))
    else:
        h0_spec = pl.BlockSpec((1, 2, D_RNN), lambda b, i: (b, 0, 0))
    return pl.pallas_call(
        _rglru_kernel,
        grid=(bsz, nc),
        in_specs=[
            main(fwd), main(bwd),
            const((2, RNN_BLOCKS, RNN_BW, RNN_BW)), const((2, 1, D_RNN)),
            const((2, RNN_BLOCKS, RNN_BW, RNN_BW)), const((2, 1, D_RNN)),
            const((2, 1, D_RNN)),
            h0_spec,
        ],
        out_specs=[main(fwd), main(bwd)],
        out_shape=[jax.ShapeDtypeStruct((bsz, seq, D_RNN), F32)] * 2,
        scratch_shapes=[
            pltpu.VMEM((per, SUBLANES, D_RNN), F32),
            pltpu.VMEM((per, SUBLANES, D_RNN), F32),
            pltpu.VMEM((SUBLANES, D_RNN), F32),
            pltpu.VMEM((SUBLANES, D_RNN), F32),
        ],
        compiler_params=pltpu.CompilerParams(
            dimension_semantics=("arbitrary", "arbitrary"), vmem_limit_bytes=VMEM_LIMIT),
        name="rglru",
    )(xc, xc, lw["lru_wa_half"], lw["lru_ba_half"], lw["lru_wi_half"], lw["lru_bi_half"], lw["lru_lam"], h0)


MLA_AHEAD = 4


def _mla_kernel(*refs, has_ctx, kc, par):
    if has_ctx:
        qt_ref, k_ref, vt_ref, kc_ref, vtc_ref, o_ref = refs
    else:
        qt_ref, k_ref, vt_ref, o_ref = refs
    seq = k_ref.shape[2]
    chunks = [(kc_ref, vtc_ref, 0, kc_ref.shape[2])] if has_ctx else []
    chunks += [(k_ref, vt_ref, c * kc, kc) for c in range(seq // kc)]

    def heads(j, carry):
        hs = [j * par + t for t in range(par)]
        qs = [qt_ref[0, h] for h in hs]
        ms = [None] * par
        accs = [None] * par
        tasks = [(t, c) for c in chunks for t in range(par)]

        def scores(task):
            t, (kr, _, off, n) = task
            return _dot(kr[0, hs[t], off:off + n, :], qs[t])

        pending = [scores(task) for task in tasks[:MLA_AHEAD]]
        for i, (t, (_, vr, off, n)) in enumerate(tasks):
            s = pending.pop(0)
            if i + MLA_AHEAD < len(tasks):
                pending.append(scores(tasks[i + MLA_AHEAD]))
            cm = jnp.max(s, axis=0, keepdims=True)
            m_new = cm if ms[t] is None else jnp.maximum(ms[t], cm)
            p = jnp.exp2(s - m_new).astype(BF16)
            pv = _dot(vr[0, hs[t], :, off:off + n], p)
            accs[t] = pv if ms[t] is None else jnp.exp2(ms[t] - m_new) * accs[t] + pv
            ms[t] = m_new
        for t, h in enumerate(hs):
            o_ref[0, h] = accs[t][:MLA_V] / accs[t][MLA_V:MLA_V + 1]
        return carry

    lax.fori_loop(0, MLA_HEADS // par, heads, 0)


def _mla(q_t, k, v_t, ctx, *, tq, kc):
    bsz, heads, _, seq = q_t.shape
    in_specs = [
        pl.BlockSpec((1, heads, MLA_HEAD_PAD, tq), lambda b, i: (b, 0, 0, i)),
        pl.BlockSpec((1, heads, seq, MLA_HEAD_PAD), lambda b, i: (b, 0, 0, 0)),
        pl.BlockSpec((1, heads, V_ROWS, seq), lambda b, i: (b, 0, 0, 0)),
    ]
    args = [q_t, k, v_t]
    if ctx is not None:
        past = ctx[0].shape[2]
        in_specs += [
            pl.BlockSpec((1, heads, past, MLA_HEAD_PAD), lambda b, i: (b, 0, 0, 0)),
            pl.BlockSpec((1, heads, V_ROWS, past), lambda b, i: (b, 0, 0, 0)),
        ]
        args += list(ctx)
    out = pl.pallas_call(
        functools.partial(_mla_kernel, has_ctx=ctx is not None, kc=kc, par=1),
        grid=(bsz, seq // tq),
        in_specs=in_specs,
        out_specs=pl.BlockSpec((1, heads, MLA_V, tq), lambda b, i: (b, 0, 0, i)),
        out_shape=jax.ShapeDtypeStruct((bsz, heads, MLA_V, seq), F32),
        compiler_params=pltpu.CompilerParams(vmem_limit_bytes=VMEM_LIMIT),
        name="mla_latent" if ctx is not None else "mla_context",
    )(*args)
    return out.reshape(bsz, heads * MLA_V, seq)


SWA_AHEAD = 2


def _swa_kernel(*refs, windowed, tq, seq):
    if windowed:
        sink_ref, qt_ref, k_ref, vt_ref, kc_ref, vtc_ref, o_ref = refs
    else:
        sink_ref, qt_ref, kc_ref, vtc_ref, o_ref = refs
    i = pl.program_id(1)
    start = i * tq
    nq = SWA_GROUPS * tq
    lane = lax.broadcasted_iota(jnp.int32, (1, nq), 1)
    if windowed:
        win = tq + 2 * WINDOW
        ws = pl.multiple_of(jnp.clip(start - WINDOW, 0, seq - win), LANES)
        kpos = ws + lax.broadcasted_iota(jnp.int32, (win, 1), 0)
        qpos = start + lax.broadcasted_iota(jnp.int32, (1, tq), 1)
        valid = jnp.abs(kpos - qpos) <= WINDOW
        kw = k_ref[0, pl.ds(ws, win), :]
    zeros = jnp.zeros((SWA_HD, nq), BF16)
    qps, sks = [], []
    for g in range(SWA_KV_HEADS):
        qg = jnp.concatenate(
            [qt_ref[0, (g * SWA_GROUPS + j) * SWA_HD:(g * SWA_GROUPS + j + 1) * SWA_HD, :] for j in range(SWA_GROUPS)],
            axis=1)
        qps.append(jnp.concatenate([qg, zeros] if g == 0 else [zeros, qg], axis=0))
        sk = jnp.zeros((1, nq), F32)
        for j in range(SWA_GROUPS):
            sk = jnp.where(lane // tq == j, sink_ref[g * SWA_GROUPS + j] * LOG2E, sk)
        sks.append(sk)

    tasks = [(g, w) for g in range(SWA_KV_HEADS) for w in ((False, True) if windowed else (False,))]

    def scores(task):
        g, w = task
        return _dot(kw if w else kc_ref[0], qps[g])

    ms = list(sks)
    accs = [None] * SWA_KV_HEADS
    pending = [scores(task) for task in tasks[:SWA_AHEAD]]
    for n, (g, w) in enumerate(tasks):
        s = pending.pop(0)
        if n + SWA_AHEAD < len(tasks):
            pending.append(scores(tasks[n + SWA_AHEAD]))
        if w:
            s = jnp.concatenate(
                [jnp.where(valid, s[:, j * tq:(j + 1) * tq], NEG_INF) for j in range(SWA_GROUPS)], axis=1)
        m_new = jnp.maximum(ms[g], jnp.max(s, axis=0, keepdims=True))
        p = jnp.exp2(s - m_new).astype(BF16)
        pv = _dot(vt_ref[0, g, :, pl.ds(ws, win)] if w else vtc_ref[0, g], p)
        accs[g] = pv if accs[g] is None else jnp.exp2(ms[g] - m_new) * accs[g] + pv
        ms[g] = m_new
    for g in range(SWA_KV_HEADS):
        o = accs[g][:SWA_HD] / (accs[g][SWA_HD:SWA_HD + 1] + jnp.exp2(sks[g] - ms[g]))
        for j in range(SWA_GROUPS):
            hd = g * SWA_GROUPS + j
            o_ref[0, hd * SWA_HD:(hd + 1) * SWA_HD, :] = o[:, j * tq:(j + 1) * tq]


def _swa(sink, qs_t, ks, vs_t, ctx, *, tq):
    bsz, _, seq = qs_t.shape
    windowed = ctx is not None
    whole_k = lambda n: pl.BlockSpec((1, n, LANES), lambda b, i: (b, 0, 0))
    whole_vt = lambda n: pl.BlockSpec((1, SWA_KV_HEADS, V_ROWS, n), lambda b, i: (b, 0, 0, 0))
    in_specs = [
        pl.BlockSpec(memory_space=pltpu.SMEM),
        pl.BlockSpec((1, SWA_WIDTH, tq), lambda b, i: (b, 0, i)),
        whole_k(seq), whole_vt(seq),
    ]
    args = [sink, qs_t, ks, vs_t]
    if windowed:
        past = ctx[0].shape[1]
        in_specs += [whole_k(past), whole_vt(past)]
        args += list(ctx)
    return pl.pallas_call(
        functools.partial(_swa_kernel, windowed=windowed, tq=tq, seq=seq),
        grid=(bsz, seq // tq),
        in_specs=in_specs,
        out_specs=pl.BlockSpec((1, SWA_WIDTH, tq), lambda b, i: (b, 0, i)),
        out_shape=jax.ShapeDtypeStruct((bsz, SWA_WIDTH, seq), F32),
        compiler_params=pltpu.CompilerParams(vmem_limit_bytes=VMEM_LIMIT),
        name="swa_latent" if windowed else "swa_context",
    )(*args)


def _silu(x):
    return x * jax.nn.sigmoid(x)


def _merge_kernel(*refs, final):
    if final:
        (x_ref, mod_ref, gn_ref, hf_ref, hb_ref, ym_ref, ys_ref, wg_ref, wr_ref, wm_ref, ws_ref, wo_ref, fn_ref,
         o_ref) = refs
    else:
        (x_ref, mod_ref, gn_ref, hf_ref, hb_ref, ym_ref, ys_ref, wg_ref, wr_ref, wm_ref, ws_ref, wo_ref,
         o_ref) = refs
    x = x_ref[0]
    mod = mod_ref[0]
    h = _norm_mod(x, gn_ref[...], mod).astype(BF16)
    g = _dot(h, wg_ref[...])
    o = 0
    g_rnn = g[:, o:o + D_RNN]
    o += D_RNN
    g_mla = g[:, o:o + MLA_WIDTH]
    o += MLA_WIDTH
    g_swa = g[:, o:o + SWA_WIDTH]
    o += SWA_WIDTH
    y_rnn = hf_ref[0] + hb_ref[0]
    y_mla = ym_ref[0].T
    y_swa = ys_ref[0].T
    b_rnn = _dot((y_rnn * _silu(g_rnn)).astype(BF16), wr_ref[...])
    b_mla = _dot((y_mla * _silu(g_mla)).astype(BF16), wm_ref[...])
    b_swa = _dot((y_swa * _silu(g_swa)).astype(BF16), ws_ref[...])
    u = (jax.nn.sigmoid(g[:, o:o + D_MODEL]) * b_rnn
         + jax.nn.sigmoid(g[:, o + D_MODEL:o + 2 * D_MODEL]) * b_mla
         + jax.nn.sigmoid(g[:, o + 2 * D_MODEL:o + 3 * D_MODEL]) * b_swa)
    out = _dot(u.astype(BF16), wo_ref[...])
    xn = x + mod[2:3] * out
    if final:
        xn = _rms(xn, fn_ref[...])
    o_ref[0] = xn


def _merge(x, mod, lw, h_f, h_b, ym_t, ys_t, final_norm, *, l, tm):
    bsz, seq, _ = x.shape
    per_batch_mod = mod.shape[0] > 1
    final = final_norm is not None
    _const_spec = functools.partial(_layer_spec, l=l)
    tok3 = lambda w: pl.BlockSpec((1, tm, w), lambda b, i: (b, i, 0))
    feat3 = lambda f: pl.BlockSpec((1, f, tm), lambda b, i: (b, 0, i))
    in_specs = [
        tok3(D_MODEL),
        pl.BlockSpec((1, 3, D_MODEL), (lambda b, i: (b, 0, 0)) if per_batch_mod else (lambda b, i: (0, 0, 0))),
        _const_spec((1, D_MODEL)),
        tok3(D_RNN), tok3(D_RNN), feat3(MLA_WIDTH), feat3(SWA_WIDTH),
        _const_spec((D_MODEL, N_GATES)),
        _const_spec((D_RNN, D_MODEL)),
        _const_spec((MLA_WIDTH, D_MODEL)),
        _const_spec((SWA_WIDTH, D_MODEL)),
        _const_spec((D_MODEL, D_MODEL)),
    ]
    args = [x, mod, lw["g_norm"], h_f, h_b, ym_t, ys_t, lw["w_gates"], lw["w_br_rnn"], lw["w_br_mla"], lw["w_br_swa"],
            lw["w_out"]]
    if final:
        in_specs.append(pl.BlockSpec((1, D_MODEL), lambda b, i: (0, 0)))
        args.append(final_norm)
    return pl.pallas_call(
        functools.partial(_merge_kernel, final=final),
        grid=(bsz, seq // tm),
        in_specs=in_specs,
        out_specs=tok3(D_MODEL),
        out_shape=jax.ShapeDtypeStruct((bsz, seq, D_MODEL), F32),
        compiler_params=pltpu.CompilerParams(vmem_limit_bytes=VMEM_LIMIT),
        name="merge_final" if final else "merge",
    )(*args)


def _split_points():
    sizes = (D_RNN, D_RNN, Q_LORA, KV_LORA, MLA_ROPE, MLA_WIDTH, SWA_WIDTH, SWA_KV_HEADS * SWA_HD,
             SWA_KV_HEADS * SWA_HD, SWA_WIDTH, 3 * D_MODEL)
    pts, acc = [0], 0
    for s in sizes:
        acc += s
        pts.append(acc)
    return pts


def _stacked_weights(g_norm, w_in, conv_w, conv_b, lru_wa, lru_ba, lru_wi, lru_bi, lru_lam, mla_q_norm, mla_w_uq,
                     mla_kv_norm, mla_w_ukv, w_br_rnn, w_br_mla, w_br_swa, w_out):
    depth = w_in.shape[0]
    p = _split_points()
    seg = lambda k: w_in[:, :, p[k]:p[k + 1]]
    (w_xr, w_grnn, w_cq, w_ckv, w_kr, w_gmla, w_qs, w_ks, w_vs, w_gswa, w_ml) = [seg(k) for k in range(11)]
    w_kr_blk = jnp.pad(w_kr, ((0, 0), (0, 0), (MLA_NOPE, MLA_HEAD_PAD - MLA_NOPE - MLA_ROPE)))
    w_row_lat = jnp.concatenate([w_xr, w_cq, w_ckv, w_kr_blk, w_ks], axis=2).astype(BF16)
    w_row_ctx = jnp.concatenate([w_xr, w_cq, w_ckv, w_kr_blk, w_ks, w_vs], axis=2).astype(BF16)
    w_t = jnp.swapaxes(jnp.concatenate([w_qs, w_vs], axis=2), 1, 2).astype(BF16)
    w_gates = jnp.concatenate([w_grnn, w_gmla, w_gswa, w_ml], axis=2).astype(BF16)
    uq = mla_w_uq.reshape(depth, Q_LORA, MLA_HEADS, MLA_NOPE + MLA_ROPE)
    uq = jnp.pad(uq, ((0, 0), (0, 0), (0, 0), (0, MLA_HEAD_PAD - MLA_NOPE - MLA_ROPE)))
    w_uq_t = jnp.swapaxes(uq.reshape(depth, Q_LORA, MLA_HEADS * MLA_HEAD_PAD), 1, 2).astype(BF16)
    ukv = mla_w_ukv.reshape(depth, KV_LORA, MLA_HEADS, MLA_NOPE + MLA_V)
    uk = jnp.pad(ukv[..., :MLA_NOPE], ((0, 0), (0, 0), (0, 0), (0, MLA_HEAD_PAD - MLA_NOPE)))
    w_uk = uk.reshape(depth, KV_LORA, MLA_HEADS * MLA_HEAD_PAD).astype(BF16)
    w_uv_t = jnp.swapaxes(ukv[..., MLA_NOPE:].reshape(depth, KV_LORA, MLA_WIDTH), 1, 2).astype(BF16)
    return dict(
        g_norm=g_norm.reshape(depth, 1, D_MODEL), w_row_lat=w_row_lat, w_row_ctx=w_row_ctx, w_t=w_t, w_gates=w_gates,
        g_q=mla_q_norm.reshape(depth, 1, Q_LORA), g_kv=mla_kv_norm.reshape(depth, 1, KV_LORA),
        w_uq_t=w_uq_t, w_uk=w_uk, w_uv_t=w_uv_t,
        conv_w=conv_w, conv_b=conv_b.reshape(depth, 1, D_RNN),
        lru_wa_half=(0.5 * lru_wa).astype(BF16), lru_ba_half=0.5 * lru_ba.reshape(depth, 2, 1, D_RNN),
        lru_wi_half=(0.5 * lru_wi).astype(BF16), lru_bi_half=0.5 * lru_bi.reshape(depth, 2, 1, D_RNN),
        lru_lam=lru_lam.reshape(depth, 2, 1, D_RNN),
        w_br_rnn=w_br_rnn.astype(BF16), w_br_mla=w_br_mla.astype(BF16), w_br_swa=w_br_swa.astype(BF16),
        w_out=w_out.astype(BF16),
    )


def _rope_tables(n_tok, rot_dim):
    rows = n_tok // GRID_W
    row = jnp.repeat(jnp.arange(rows, dtype=F32), GRID_W)
    col = jnp.tile(jnp.arange(GRID_W, dtype=F32), rows)
    half = rot_dim // 2
    freqs = ROPE_BASE ** (-jnp.arange(0, half, 2, dtype=F32) / half)
    ang_r = row[:, None] * freqs[None, :]
    ang_c = col[:, None] * freqs[None, :]
    ang = jnp.concatenate([ang_r, ang_r, ang_c, ang_c], axis=-1)
    q = rot_dim // 4
    sign = jnp.concatenate([-jnp.ones((q,), F32), jnp.ones((q,), F32)] * 2)
    return jnp.cos(ang), jnp.sin(ang) * sign


def _rope_inputs(n_tok):
    cos_m, sin_m = _rope_tables(n_tok, MLA_ROPE)
    cos_s, sin_s = _rope_tables(n_tok, SWA_HD)
    pad_m = ((0, 0), (MLA_NOPE, MLA_HEAD_PAD - MLA_NOPE - MLA_ROPE))
    return (cos_m.T, sin_m.T, jnp.pad(cos_m, pad_m, constant_values=1.0), jnp.pad(sin_m, pad_m),
            cos_s.T, sin_s.T, jnp.tile(cos_s, (1, SWA_KV_HEADS)), jnp.tile(sin_s, (1, SWA_KV_HEADS)))


def _tile(n, pref):
    return pref if n % pref == 0 else n


def _context_layer(l, x, mod, lw, sink, final_norm):
    bsz, seq, _ = x.shape
    (xc, q_t, k, v_t, qs_t, ks, vs_t, ckv, kr_blk, ks_raw, vs_raw) = _proj(
        x, mod, lw, None, l=l, latent=False, tm=_tile(seq, 256))
    h_f, h_b = _rglru(xc, jnp.zeros((bsz, 2, D_RNN), F32), lw, l=l, tl=_tile(seq, 256))
    ym_t = _mla(q_t, k, v_t, None, tq=_tile(seq, 256), kc=_tile(seq, 512))
    ys_t = _swa(sink, qs_t, ks, vs_t, None, tq=_tile(seq, 256))
    x = _merge(x, mod, lw, h_f, h_b, ym_t, ys_t, final_norm, l=l, tm=_tile(seq, 256))
    h_state = jnp.stack([h_f[:, -1], h_b[:, 0]], axis=1)
    k_rope = kr_blk[:, :, MLA_NOPE:MLA_NOPE + MLA_ROPE]
    return (x, ckv, k_rope, ks_raw.reshape(bsz, seq, SWA_KV_HEADS, SWA_HD),
            vs_raw.reshape(bsz, seq, SWA_KV_HEADS, SWA_HD), h_state)


def _latent_layer(l, x, mod, lw, sink, rope, cache, final_norm):
    bsz, seq, _ = x.shape
    ckv_all, kr_blk_all, ks_c_all, vs_c_t_all, state = cache
    xc, q_t, k, v_t, qs_t, ks, vs_t = _proj(x, mod, lw, rope, l=l, latent=True, tm=_tile(seq, 512))
    kc, vc_t = _cache_proj(ckv_all, kr_blk_all, lw, l=l)
    h_f, h_b = _rglru(xc, state, lw, l=l, tl=_tile(seq, 256))
    ym_t = _mla(q_t, k, v_t, (kc, vc_t), tq=_tile(seq, 512), kc=_tile(seq, 256))
    ys_t = _swa(sink, qs_t, ks, vs_t, (ks_c_all[:, l], vs_c_t_all[:, l]), tq=_tile(seq, 256))
    return _merge(x, mod, lw, h_f, h_b, ym_t, ys_t, final_norm, l=l, tm=_tile(seq, 512))


def kernel(x_prompt, x_sample, cache_mla_ckv, cache_mla_krope, cache_swa_k, cache_swa_v, state_rglru, c, c_ctx, w_mod, b_mod, g_norm, w_in, conv_w, conv_b, lru_wa, lru_ba, lru_wi, lru_bi, lru_lam, mla_q_norm, mla_w_uq, mla_kv_norm, mla_w_ukv, swa_sink, w_br_rnn, w_br_mla, w_br_swa, w_out, final_norm):
    depth = w_in.shape[0]
    dec_b = x_sample.shape[0]
    n_cond = -(-(dec_b + 1) // SUBLANES) * SUBLANES
    conds = jnp.concatenate([c, c_ctx[None, :], jnp.zeros((n_cond - dec_b - 1, D_MODEL), F32)], axis=0)
    mod = _modulation(conds, w_mod, b_mod).reshape(depth, n_cond, 3, D_MODEL)
    fn = final_norm.reshape(1, D_MODEL)
    lw = _stacked_weights(g_norm, w_in, conv_w, conv_b, lru_wa, lru_ba, lru_wi, lru_bi, lru_lam, mla_q_norm,
                          mla_w_uq, mla_kv_norm, mla_w_ukv, w_br_rnn, w_br_mla, w_br_swa, w_out)

    xp = x_prompt
    ckvs, krs, sks, svs, hs = [], [], [], [], []
    for l in range(depth):
        xp, ckv, kr, sk, sv, hst = _context_layer(l, xp, mod[l, dec_b:dec_b + 1], lw, swa_sink[l],
                                                  fn if l == depth - 1 else None)
        ckvs.append(ckv)
        krs.append(kr)
        sks.append(sk)
        svs.append(sv)
        hs.append(hst)

    rope = _rope_inputs(x_sample.shape[1])
    past = cache_mla_ckv.shape[2]
    cache = (
        cache_mla_ckv,
        jnp.pad(cache_mla_krope, ((0, 0), (0, 0), (0, 0), (MLA_NOPE, MLA_HEAD_PAD - MLA_NOPE - MLA_ROPE))),
        cache_swa_k.reshape(dec_b, depth, past, SWA_KV_HEADS * SWA_HD).astype(BF16),
        jnp.concatenate([jnp.transpose(cache_swa_v, (0, 1, 3, 4, 2)).astype(BF16),
                         jnp.ones((dec_b, depth, SWA_KV_HEADS, ONES_ROWS, past), BF16)], axis=3),
        state_rglru,
    )
    xs = x_sample
    for l in range(depth):
        xs = _latent_layer(l, xs, mod[l, :dec_b], lw, swa_sink[l], rope, cache, fn if l == depth - 1 else None)
    return (xp, xs, jnp.stack(ckvs, axis=1), jnp.stack(krs, axis=1), jnp.stack(sks, axis=1),
            jnp.stack(svs, axis=1), jnp.stack(hs, axis=1))
```
